```python
import math
import jax, jax.numpy as jnp
from jax import lax
import numpy as np

D_MODEL = 1024
BATCH = 8
SEQ = 4096
DEPTH = 1

CHUNK = 64
Q_BLOCK = 128
EPS = 1e-6
NEG_INF = -1e30

MLA_HEADS = 8
MLA_NOPE_DIM = 64
MLA_ROPE_DIM = 32
MLA_V_DIM = 64
MLA_Q_RANK = 256
MLA_KV_RANK = 128
ROPE_THETA = 10000.0

DIFF_HEADS = 4
DIFF_HEAD_DIM = 64
DIFF_V_DIM = 2 * DIFF_HEAD_DIM

REL_BUCKETS = 32
REL_MAX_DIST = 128

FFN_HIDDEN = -(-8 * D_MODEL // (3 * 256)) * 256

MIX_WIDTH = MLA_HEADS * MLA_V_DIM + DIFF_HEADS * DIFF_V_DIM
DIFF_QK_COLS = DIFF_HEADS * 2 * DIFF_HEAD_DIM
DIFF_V_COLS = DIFF_HEADS * DIFF_V_DIM
IN_SPLITS = [MLA_Q_RANK, MLA_KV_RANK, MLA_ROPE_DIM, DIFF_QK_COLS, DIFF_QK_COLS, DIFF_V_COLS]
IN_COLS = sum(IN_SPLITS)
IN_OFFSETS = [int(v) for v in np.cumsum(IN_SPLITS)[:-1]]

kernel_name = "hybrid_mla_diffattn_chunk_causal_layer"


def _rmsnorm(x, g):
    xf = x.astype(jnp.float32)
    y = xf * lax.rsqrt(jnp.mean(xf * xf, axis=-1, keepdims=True) + EPS)
    return (y * g.astype(jnp.float32)).astype(x.dtype)


def _rope(x, pos):
    half = x.shape[-1] // 2
    inv_freq = ROPE_THETA ** (-jnp.arange(half, dtype=jnp.float32) / half)
    ang = pos.astype(jnp.float32)[:, None] * inv_freq[None, :]
    cos = jnp.cos(ang)[None, :, None, :].astype(x.dtype)
    sin = jnp.sin(ang)[None, :, None, :].astype(x.dtype)
    x1, x2 = x[..., :half], x[..., half:]
    return jnp.concatenate([x1 * cos - x2 * sin, x2 * cos + x1 * sin], axis=-1)


def _to_blocks(t):
    b, s, h, d = t.shape
    return t.reshape(b, s // Q_BLOCK, Q_BLOCK, h, d).transpose(1, 0, 3, 2, 4)


def _from_blocks(t):
    nb, b, h, qb, d = t.shape
    return t.transpose(1, 0, 3, 2, 4).reshape(b, nb * qb, h, d)


def _block_positions(blk, seq):
    q_pos = blk * Q_BLOCK + jnp.arange(Q_BLOCK)
    k_pos = jnp.arange(seq)
    mask = (k_pos[None, :] // CHUNK) <= (q_pos[:, None] // CHUNK)
    return mask, q_pos, k_pos


def _t5_bucket(rel):
    nb = REL_BUCKETS // 2
    max_exact = nb // 2
    ret = (rel > 0).astype(jnp.int32) * nb
    n = jnp.abs(rel)
    nf = jnp.maximum(n, 1).astype(jnp.float32)
    large = max_exact + (jnp.log(nf / max_exact) / math.log(REL_MAX_DIST / max_exact)
                         * (nb - max_exact)).astype(jnp.int32)
    large = jnp.minimum(large, nb - 1)
    return ret + jnp.where(n < max_exact, n, large)


def _mla_attention(q_nope, q_rope, k_nope, k_rope, v):
    seq = q_nope.shape[1]
    scale = (MLA_NOPE_DIM + MLA_ROPE_DIM) ** -0.5
    kn = k_nope.transpose(0, 2, 1, 3)
    vv = v.transpose(0, 2, 1, 3)

    def one_block(args):
        qn, qr, blk = args
        s = (jnp.einsum('bhqd,bhkd->bhqk', qn, kn)
             + jnp.einsum('bhqd,bkd->bhqk', qr, k_rope)).astype(jnp.float32) * scale
        mask, _, _ = _block_positions(blk, seq)
        p = jax.nn.softmax(jnp.where(mask[None, None], s, NEG_INF), axis=-1)
        return jnp.einsum('bhqk,bhkd->bhqd', p.astype(vv.dtype), vv)

    out = lax.map(one_block, (_to_blocks(q_nope), _to_blocks(q_rope),
                              jnp.arange(seq // Q_BLOCK)))
    return _from_blocks(out)


def _diff_attention(q1, q2, k1, k2, v, lam, rel_bias):
    seq = q1.shape[1]
    scale = DIFF_HEAD_DIM ** -0.5
    k1t = k1.transpose(0, 2, 1, 3)
    k2t = k2.transpose(0, 2, 1, 3)
    vv = v.transpose(0, 2, 1, 3)

    def one_block(args):
        qa, qb, blk = args
        mask, q_pos, k_pos = _block_positions(blk, seq)
        bucket = _t5_bucket(k_pos[None, :] - q_pos[:, None])
        bias = rel_bias[bucket].astype(jnp.float32).transpose(2, 0, 1)[None]
        s1 = jnp.einsum('bhqd,bhkd->bhqk', qa, k1t).astype(jnp.float32) * scale + bias
        s2 = jnp.einsum('bhqd,bhkd->bhqk', qb, k2t).astype(jnp.float32) * scale + bias
        m = mask[None, None]
        p = (jax.nn.softmax(jnp.where(m, s1, NEG_INF), axis=-1)
             - lam * jax.nn.softmax(jnp.where(m, s2, NEG_INF), axis=-1))
        return jnp.einsum('bhqk,bhkd->bhqd', p.astype(vv.dtype), vv)

    out = lax.map(one_block, (_to_blocks(q1), _to_blocks(q2), jnp.arange(seq // Q_BLOCK)))
    return _from_blocks(out)


def setup_inputs(seed: int = 0) -> dict:
    key = jax.random.key(seed)
    ks = jax.random.split(key, 20)
    nrm = lambda k, shape, fan_in: jax.random.normal(k, shape, jnp.float32) * fan_in ** -0.5
    gain = lambda k, shape: 1.0 + 0.01 * jax.random.normal(k, shape, jnp.float32)
    L = DEPTH
    return {
        "x": jax.random.normal(ks[0], (BATCH, SEQ, D_MODEL), jnp.float32),
        "w_in": nrm(ks[1], (L, D_MODEL, IN_COLS), D_MODEL),
        "g_attn": gain(ks[2], (L, D_MODEL)),
        "g_q_a": gain(ks[3], (L, MLA_Q_RANK)),
        "w_q_b": nrm(ks[4], (L, MLA_Q_RANK, MLA_HEADS * (MLA_NOPE_DIM + MLA_ROPE_DIM)), MLA_Q_RANK),
        "g_kv_a": gain(ks[5], (L, MLA_KV_RANK)),
        "w_kv_b": nrm(ks[6], (L, MLA_KV_RANK, MLA_HEADS * (MLA_NOPE_DIM + MLA_V_DIM)), MLA_KV_RANK),
        "lam_q1": 0.1 * jax.random.normal(ks[7], (L, DIFF_HEAD_DIM), jnp.float32),
        "lam_k1": 0.1 * jax.random.normal(ks[8], (L, DIFF_HEAD_DIM), jnp.float32),
        "lam_q2": 0.1 * jax.random.normal(ks[9], (L, DIFF_HEAD_DIM), jnp.float32),
        "lam_k2": 0.1 * jax.random.normal(ks[10], (L, DIFF_HEAD_DIM), jnp.float32),
        "g_subln": gain(ks[11], (L, DIFF_V_DIM)),
        "rel_bias": 0.5 * jax.random.normal(ks[12], (REL_BUCKETS, DIFF_HEADS), jnp.float32),
        "w_o": nrm(ks[13], (L, MIX_WIDTH, D_MODEL), MIX_WIDTH),
        "g_ffn": gain(ks[14], (L, D_MODEL)),
        "w_ffn_gate": nrm(ks[15], (L, D_MODEL, FFN_HIDDEN), D_MODEL),
        "w_ffn_up": nrm(ks[16], (L, D_MODEL, FFN_HIDDEN), D_MODEL),
        "w_ffn_down": nrm(ks[17], (L, FFN_HIDDEN, D_MODEL), FFN_HIDDEN),
        "g_final": gain(ks[18], (D_MODEL,)),
    }


def reference(x, w_in, g_attn, g_q_a, w_q_b, g_kv_a, w_kv_b, lam_q1, lam_k1, lam_q2,
              lam_k2, g_subln, rel_bias, w_o, g_ffn, w_ffn_gate, w_ffn_up, w_ffn_down,
              g_final):
    b, s, _ = x.shape
    pos = jnp.arange(s)
    for l in range(DEPTH):
        h = _rmsnorm(x, g_attn[l])
        proj = h @ w_in[l]
        c_q, c_kv, k_r, d_q, d_k, d_v = jnp.split(proj, IN_OFFSETS, axis=-1)

        q = (_rmsnorm(c_q, g_q_a[l]) @ w_q_b[l]).reshape(b, s, MLA_HEADS, MLA_NOPE_DIM + MLA_ROPE_DIM)
        kv = (_rmsnorm(c_kv, g_kv_a[l]) @ w_kv_b[l]).reshape(b, s, MLA_HEADS, MLA_NOPE_DIM + MLA_V_DIM)
        q_nope, q_rope = q[..., :MLA_NOPE_DIM], _rope(q[..., MLA_NOPE_DIM:], pos)
        k_nope, v_mla = kv[..., :MLA_NOPE_DIM], kv[..., MLA_NOPE_DIM:]
        k_rope = _rope(k_r[:, :, None, :], pos)[:, :, 0, :]
        out_a = _mla_attention(q_nope, q_rope, k_nope, k_rope, v_mla)
        out_a = out_a.reshape(b, s, MLA_HEADS * MLA_V_DIM)

        dq = d_q.reshape(b, s, DIFF_HEADS, 2, DIFF_HEAD_DIM)
        dk = d_k.reshape(b, s, DIFF_HEADS, 2, DIFF_HEAD_DIM)
        dv = d_v.reshape(b, s, DIFF_HEADS, DIFF_V_DIM)
        lam_init = 0.8 - 0.6 * math.exp(-0.3 * l)
        lam = (jnp.exp(jnp.sum(lam_q1[l].astype(jnp.float32) * lam_k1[l].astype(jnp.float32)))
               - jnp.exp(jnp.sum(lam_q2[l].astype(jnp.float32) * lam_k2[l].astype(jnp.float32)))
               + lam_init)
        out_b = _diff_attention(dq[..., 0, :], dq[..., 1, :], dk[..., 0, :], dk[..., 1, :],
                                dv, lam, rel_bias)
        out_b = _rmsnorm(out_b, g_subln[l]) * (1.0 - lam_init)
        out_b = out_b.reshape(b, s, DIFF_HEADS * DIFF_V_DIM)

        mixed = jnp.concatenate([out_a, out_b], axis=-1)
        x = x + mixed @ w_o[l]

        h = _rmsnorm(x, g_ffn[l])
        x = x + (jax.nn.silu(h @ w_ffn_gate[l]) * (h @ w_ffn_up[l])) @ w_ffn_down[l]
    return _rmsnorm(x, g_final)
```

```python
import functools
import math

import numpy as np
import jax
import jax.numpy as jnp
from jax import lax
from jax.experimental import pallas as pl
from jax.experimental.pallas import tpu as pltpu

D_MODEL = 1024
CHUNK = 64
EPS = 1e-6
NEG_INF = -1e30

MLA_HEADS = 8
MLA_NOPE_DIM = 64
MLA_ROPE_DIM = 32
MLA_V_DIM = 64
MLA_Q_RANK = 256
MLA_KV_RANK = 128
ROPE_THETA = 10000.0

DIFF_HEADS = 4
DIFF_HEAD_DIM = 64
DIFF_V_DIM = 2 * DIFF_HEAD_DIM

REL_BUCKETS = 32
REL_MAX_DIST = 128

FFN_HIDDEN = 2816
LAM_INIT = 0.8 - 0.6 * math.exp(-0.3 * 0)

LOG2E = 1.4426950408889634
MLA_SCALE = (MLA_NOPE_DIM + MLA_ROPE_DIM) ** -0.5
DIFF_SCALE = DIFF_HEAD_DIM ** -0.5

LANES = 128
PROJ_ROWS = 512
ATTN_TILE = 512
FFN_ROWS = 512
FFN_CHUNKS = ((0, 1024), (1024, 2048), (2048, FFN_HIDDEN))
VMEM_LIMIT_BYTES = 56 * 1024 * 1024

_BF16 = jnp.bfloat16
_F32 = jnp.float32


def _rms(x, g):
    return x * lax.rsqrt(jnp.mean(x * x, axis=-1, keepdims=True) + EPS) * g


def _dot(a, b):
    return jnp.dot(a, b, preferred_element_type=_F32)


def _dot_nt(a, b):
    return lax.dot_general(a, b, (((1,), (1,)), ((), ())), preferred_element_type=_F32)


def _t5_bucket(rel):
    nb = REL_BUCKETS // 2
    max_exact = nb // 2
    ret = (rel > 0).astype(jnp.int32) * nb
    n = jnp.abs(rel)
    nf = jnp.maximum(n, 1).astype(jnp.float32)
    large = max_exact + (jnp.log(nf / max_exact) / math.log(REL_MAX_DIST / max_exact)
                         * (nb - max_exact)).astype(jnp.int32)
    large = jnp.minimum(large, nb - 1)
    return ret + jnp.where(n < max_exact, n, large)


def _bias_kernel(rel_ref, bdiag_ref, bcorner_ref, adiag_ref, acorner_ref, mask_ref):
    t = ATTN_TILE
    row = lax.broadcasted_iota(jnp.int32, (t, t), 0) // CHUNK
    col = lax.broadcasted_iota(jnp.int32, (t, t), 1) // CHUNK
    mask = jnp.where(col <= row, 0.0, NEG_INF).astype(_F32)
    mask_ref[...] = mask
    bd = bdiag_ref[...]
    bc = bcorner_ref[...]
    far_bucket = REL_BUCKETS // 2 - 1
    for h in range(DIFF_HEADS):
        far = rel_ref[far_bucket, h]
        vd = jnp.zeros((t, t), _F32)
        vc = jnp.zeros((LANES, LANES), _F32)
        for b in range(REL_BUCKETS):
            val = rel_ref[b, h]
            vd = jnp.where(bd == b, val, vd)
            vc = jnp.where(bc == b, val, vc)
        adiag_ref[h] = (vd - far) * LOG2E + mask
        acorner_ref[h] = (vc - far) * LOG2E


def _bias_tables(rel_bias):
    t = ATTN_TILE
    i = jnp.arange(t)
    bdiag = _t5_bucket(i[None, :] - i[:, None])
    c = jnp.arange(LANES)
    bcorner = _t5_bucket(c[None, :] - c[:, None] - LANES)
    vmem = pl.BlockSpec(memory_space=pltpu.VMEM)
    return pl.pallas_call(
        _bias_kernel,
        out_shape=(jax.ShapeDtypeStruct((DIFF_HEADS, t, t), _F32),
                   jax.ShapeDtypeStruct((DIFF_HEADS, LANES, LANES), _F32),
                   jax.ShapeDtypeStruct((t, t), _F32)),
        in_specs=[pl.BlockSpec(memory_space=pltpu.SMEM), vmem, vmem],
        out_specs=(vmem, vmem, vmem),
        name="bias_tables",
    )(rel_bias, bdiag, bcorner)


def _rope_blocks(x, cos, sin, first_half):
    n = x.shape[-1]
    swapped = jnp.where(first_half, pltpu.roll(x, n - MLA_ROPE_DIM // 2, 1),
                        pltpu.roll(x, MLA_ROPE_DIM // 2, 1))
    return x * cos + swapped * sin


def _proj_kernel(x_ref, gattn_ref, win_ref, gq_ref, wq_ref, gkv_ref, wk_ref, wv_ref, cos_ref, sin_ref,
                 qm_ref, km_ref, vm_ref, dq_ref, dk_ref, dv_ref):
    rows = x_ref.shape[0]
    h = _rms(x_ref[...], gattn_ref[...]).astype(_BF16)
    proj = _dot(h, win_ref[...])
    cq = _rms(proj[:, 0:256], gq_ref[...]).astype(_BF16)
    ckv = _rms(proj[:, 256:384], gkv_ref[...]).astype(_BF16)

    cos = cos_ref[...]
    sin = sin_ref[...]
    lane = lax.broadcasted_iota(jnp.int32, (rows, LANES), 1)
    first = (lane >= MLA_NOPE_DIM) & (lane < MLA_NOPE_DIM + MLA_ROPE_DIM // 2)
    cos8 = jnp.concatenate([cos] * MLA_HEADS, axis=1)
    sin8 = jnp.concatenate([sin] * MLA_HEADS, axis=1)
    first8 = jnp.concatenate([first] * MLA_HEADS, axis=1)

    q = _dot(cq, wq_ref[...])
    q = _rope_blocks(q, cos8, sin8, first8)
    qm_ref[...] = (q * (MLA_SCALE * LOG2E)).astype(_BF16)

    kr = _rope_blocks(proj[:, 384:512], cos, sin, first)
    kk = _dot(ckv, wk_ref[...])
    km_ref[...] = (kk + jnp.concatenate([kr] * MLA_HEADS, axis=1)).astype(_BF16)
    vm_ref[...] = _dot(ckv, wv_ref[...]).astype(_BF16)

    dq_ref[...] = (proj[:, 512:1024] * (DIFF_SCALE * LOG2E)).astype(_BF16)
    dk_ref[...] = proj[:, 1024:1536].astype(_BF16)
    dv_ref[...] = proj[:, 1536:2048].astype(_BF16)


def _projections(x2d, g_attn, w_in_p, g_q, w_q_p, g_kv, w_k_p, w_v_p, cos_t, sin_t, seq):
    n = x2d.shape[0]
    r = PROJ_ROWS
    tiles_per_seq = seq // r

    def rows(width):
        return pl.BlockSpec((r, width), lambda i: (i, 0))

    def whole(a):
        return pl.BlockSpec(a.shape, lambda i: (0, 0))

    pos = pl.BlockSpec((r, LANES), lambda i: (i % tiles_per_seq, 0))
    outs = (1024, 1024, 512, 512, 512, 512)
    return pl.pallas_call(
        _proj_kernel,
        grid=(n // r,),
        in_specs=[rows(D_MODEL), whole(g_attn), whole(w_in_p), whole(g_q), whole(w_q_p), whole(g_kv),
                  whole(w_k_p), whole(w_v_p), pos, pos],
        out_specs=tuple(rows(w) for w in outs),
        out_shape=tuple(jax.ShapeDtypeStruct((n, w), _BF16) for w in outs),
        compiler_params=pltpu.CompilerParams(dimension_semantics=("parallel",),
                                             vmem_limit_bytes=VMEM_LIMIT_BYTES),
        name="projections",
    )(x2d, g_attn, w_in_p, g_q, w_q_p, g_kv, w_k_p, w_v_p, cos_t, sin_t)


def _online_softmax_step(s, v, m_ref, l_ref, acc_ref, h):
    m_prev = m_ref[h]
    m_new = jnp.maximum(m_prev, jnp.max(s, axis=-1, keepdims=True))
    alpha = jnp.exp2(m_prev - m_new)
    p = jnp.exp2(s - m_new)
    l_ref[h] = alpha * l_ref[h] + jnp.sum(p, axis=-1, keepdims=True)
    acc_ref[h] = alpha * acc_ref[h] + _dot(p.astype(_BF16), v)
    m_ref[h] = m_new


def _attn_kernel(qi_tab, ki_tab,
                 qm_ref, km_ref, vm_ref, dq_ref, dk_ref, dv_ref,
                 adiag_ref, acorner_ref, mask_ref, lam_ref, gsub_ref,
                 out_ref,
                 m_a, l_a, acc_a, m_d, l_d, acc_d):
    t = ATTN_TILE
    step = pl.program_id(1)
    qi = qi_tab[step]
    ki = ki_tab[step]

    @pl.when(ki == 0)
    def _init():
        m_a[...] = jnp.full(m_a.shape, -jnp.inf, _F32)
        l_a[...] = jnp.zeros(l_a.shape, _F32)
        acc_a[...] = jnp.zeros(acc_a.shape, _F32)
        m_d[...] = jnp.full(m_d.shape, -jnp.inf, _F32)
        l_d[...] = jnp.zeros(l_d.shape, _F32)
        acc_d[...] = jnp.zeros(acc_d.shape, _F32)

    def add_corner(blk, corner):
        top = blk[:LANES]
        top = jnp.concatenate([top[:, :t - LANES], top[:, t - LANES:] + corner], axis=1)
        return jnp.concatenate([top, blk[LANES:]], axis=0)

    def tile_step(kind):
        for h in range(MLA_HEADS):
            blk = slice(h * LANES, (h + 1) * LANES)
            s = _dot_nt(qm_ref[0, :, blk], km_ref[0, :, blk])
            if kind == "diag":
                s = s + mask_ref[...]
            pair = slice((h // 2) * LANES, (h // 2 + 1) * LANES)
            _online_softmax_step(s, vm_ref[0, :, pair], m_a, l_a, acc_a, h)

        lane = lax.broadcasted_iota(jnp.int32, (t, LANES), 1)
        for h in range(DIFF_HEADS):
            blk = slice(h * LANES, (h + 1) * LANES)
            q = dq_ref[0, :, blk]
            zero = jnp.zeros_like(q)
            q12 = jnp.concatenate([jnp.where(lane < DIFF_HEAD_DIM, q, zero),
                                   jnp.where(lane >= DIFF_HEAD_DIM, q, zero)], axis=0)
            s = _dot_nt(q12, dk_ref[0, :, blk])
            if kind == "diag":
                a = adiag_ref[h]
                s = s + jnp.concatenate([a, a], axis=0)
            elif kind == "sub":
                c = acorner_ref[h]
                s = jnp.concatenate([add_corner(s[:t], c), add_corner(s[t:], c)], axis=0)
            _online_softmax_step(s, dv_ref[0, :, blk], m_d, l_d, acc_d, h)

    @pl.when(ki < qi - 1)
    def _far():
        tile_step("far")

    @pl.when(ki == qi - 1)
    def _sub():
        tile_step("sub")

    @pl.when(ki == qi)
    def _diag():
        tile_step("diag")
        lane = lax.broadcasted_iota(jnp.int32, (t, LANES), 1)
        for j in range(MLA_HEADS // 2):
            o0 = acc_a[2 * j] * (1.0 / l_a[2 * j])
            o1 = acc_a[2 * j + 1] * (1.0 / l_a[2 * j + 1])
            out_ref[0, :, j * LANES:(j + 1) * LANES] = jnp.where(lane < MLA_V_DIM, o0, o1).astype(_BF16)
        lam = (jnp.exp(jnp.sum(lam_ref[0:1, :] * lam_ref[1:2, :], axis=-1, keepdims=True))
               - jnp.exp(jnp.sum(lam_ref[2:3, :] * lam_ref[3:4, :], axis=-1, keepdims=True))
               + LAM_INIT)
        base = MLA_HEADS * MLA_V_DIM
        for h in range(DIFF_HEADS):
            acc = acc_d[h]
            inv = 1.0 / l_d[h]
            o = acc[:t] * inv[:t] - lam * (acc[t:] * inv[t:])
            y = _rms(o, gsub_ref[...]) * (1.0 - LAM_INIT)
            out_ref[0, :, base + h * LANES:base + (h + 1) * LANES] = y.astype(_BF16)


def _attention(qm, km, vm, dq, dk, dv, adiag, acorner, mask, lam_vecs, g_sub):
    b, s, _ = qm.shape
    t = ATTN_TILE
    nq = s // t
    pairs = [(qi, ki) for qi in range(nq) for ki in range(qi + 1)]
    qi_tab = jnp.asarray(np.array([p[0] for p in pairs], np.int32))
    ki_tab = jnp.asarray(np.array([p[1] for p in pairs], np.int32))

    def q_spec(width):
        return pl.BlockSpec((1, t, width), lambda bi, p, qt, kt: (bi, qt[p], 0))

    def k_spec(width):
        return pl.BlockSpec((1, t, width), lambda bi, p, qt, kt: (bi, kt[p], 0))

    def const(a):
        nd = a.ndim
        return pl.BlockSpec(a.shape, lambda bi, p, qt, kt: (0,) * nd)

    grid_spec = pltpu.PrefetchScalarGridSpec(
        num_scalar_prefetch=2,
        grid=(b, len(pairs)),
        in_specs=[q_spec(1024), k_spec(1024), k_spec(512), q_spec(512), k_spec(512), k_spec(512),
                  const(adiag), const(acorner), const(mask), const(lam_vecs), const(g_sub)],
        out_specs=q_spec(1024),
        scratch_shapes=[
            pltpu.VMEM((MLA_HEADS, t, 1), _F32), pltpu.VMEM((MLA_HEADS, t, 1), _F32),
            pltpu.VMEM((MLA_HEADS, t, LANES), _F32),
            pltpu.VMEM((DIFF_HEADS, 2 * t, 1), _F32), pltpu.VMEM((DIFF_HEADS, 2 * t, 1), _F32),
            pltpu.VMEM((DIFF_HEADS, 2 * t, LANES), _F32),
        ],
    )
    return pl.pallas_call(
        _attn_kernel,
        grid_spec=grid_spec,
        out_shape=jax.ShapeDtypeStruct((b, s, 1024), _BF16),
        compiler_params=pltpu.CompilerParams(dimension_semantics=("parallel", "arbitrary"),
                                             vmem_limit_bytes=VMEM_LIMIT_BYTES),
        name="attention",
    )(qi_tab, ki_tab, qm, km, vm, dq, dk, dv, adiag, acorner, mask, lam_vecs, g_sub)


def _ffn_kernel(mixed_ref, x_ref, wo_ref, gffn_ref, wg_ref, wu_ref, wd_ref, gfin_ref, out_ref):
    x1 = x_ref[...] + _dot(mixed_ref[...], wo_ref[...])
    h = _rms(x1, gffn_ref[...]).astype(_BF16)
    y = jnp.zeros_like(x1)
    for c0, c1 in FFN_CHUNKS:
        g = _dot(h, wg_ref[:, c0:c1])
        u = _dot(h, wu_ref[:, c0:c1])
        act = (g * (1.0 / (1.0 + jnp.exp(-g))) * u).astype(_BF16)
        y = y + _dot(act, wd_ref[c0:c1, :])
    out_ref[...] = _rms(x1 + y, gfin_ref[...])


def _out_ffn(mixed2d, x2d, w_o, g_ffn, w_g, w_u, w_d, g_fin):
    n = x2d.shape[0]
    r = FFN_ROWS

    def rows(width):
        return pl.BlockSpec((r, width), lambda i: (i, 0))

    def resident(a):
        return pl.BlockSpec(a.shape, lambda i: (0, 0), pipeline_mode=pl.Buffered(1))

    return pl.pallas_call(
        _ffn_kernel,
        grid=(n // r,),
        in_specs=[rows(1024), rows(D_MODEL), resident(w_o), resident(g_ffn), resident(w_g), resident(w_u),
                  resident(w_d), resident(g_fin)],
        out_specs=rows(D_MODEL),
        out_shape=jax.ShapeDtypeStruct((n, D_MODEL), _F32),
        compiler_params=pltpu.CompilerParams(dimension_semantics=("parallel",),
                                             vmem_limit_bytes=VMEM_LIMIT_BYTES),
        name="out_ffn",
    )(mixed2d, x2d, w_o, g_ffn, w_g, w_u, w_d, g_fin)


def _rope_tables(seq):
    half = MLA_ROPE_DIM // 2
    inv_freq = ROPE_THETA ** (-jnp.arange(half, dtype=jnp.float32) / half)
    ang = jnp.arange(seq).astype(jnp.float32)[:, None] * inv_freq[None, :]
    cos = jnp.cos(ang)
    sin = jnp.sin(ang)
    one = jnp.ones((seq, MLA_NOPE_DIM), _F32)
    zero = jnp.zeros((seq, MLA_NOPE_DIM), _F32)
    pad = LANES - MLA_NOPE_DIM - MLA_ROPE_DIM
    cos_t = jnp.concatenate([one, cos, cos, one[:, :pad]], axis=1)
    sin_t = jnp.concatenate([zero, -sin, sin, zero[:, :pad]], axis=1)
    return cos_t, sin_t


def kernel(x, w_in, g_attn, g_q_a, w_q_b, g_kv_a, w_kv_b, lam_q1, lam_k1, lam_q2, lam_k2, g_subln,
           rel_bias, w_o, g_ffn, w_ffn_gate, w_ffn_up, w_ffn_down, g_final):
    b, s, d = x.shape
    assert d == D_MODEL and s % ATTN_TILE == 0 and s % PROJ_ROWS == 0 and (b * s) % FFN_ROWS == 0
    assert w_in.shape[0] == 1, "single layer"

    w = w_in[0]
    zc = lambda n: jnp.zeros((D_MODEL, n), w.dtype)
    pad = LANES - MLA_NOPE_DIM - MLA_ROPE_DIM
    w_in_p = jnp.concatenate(
        [w[:, 0:384], zc(MLA_NOPE_DIM), w[:, 384:416], zc(pad), w[:, 416:]], axis=1).astype(_BF16)
    wq = w_q_b[0].reshape(MLA_Q_RANK, MLA_HEADS, MLA_NOPE_DIM + MLA_ROPE_DIM)
    w_q_p = jnp.concatenate([wq, jnp.zeros((MLA_Q_RANK, MLA_HEADS, pad), wq.dtype)],
                            axis=2).reshape(MLA_Q_RANK, MLA_HEADS * LANES).astype(_BF16)
    wkv = w_kv_b[0].reshape(MLA_KV_RANK, MLA_HEADS, MLA_NOPE_DIM + MLA_V_DIM)
    w_k_p = jnp.concatenate([wkv[:, :, :MLA_NOPE_DIM],
                             jnp.zeros((MLA_KV_RANK, MLA_HEADS, LANES - MLA_NOPE_DIM), wkv.dtype)],
                            axis=2).reshape(MLA_KV_RANK, MLA_HEADS * LANES).astype(_BF16)
    w_v_p = wkv[:, :, MLA_NOPE_DIM:].reshape(MLA_KV_RANK, MLA_HEADS * MLA_V_DIM).astype(_BF16)

    cos_t, sin_t = _rope_tables(s)
    x2d = x.reshape(b * s, d)
    qm, km, vm, dq, dk, dv = _projections(
        x2d, g_attn[0][None, :], w_in_p, g_q_a[0][None, :], w_q_p, g_kv_a[0][None, :], w_k_p, w_v_p,
        cos_t, sin_t, s)

    adiag, acorner, mask = _bias_tables(rel_bias)
    lam_vecs = jnp.stack([lam_q1[0], lam_k1[0], lam_q2[0], lam_k2[0]]).astype(_F32)
    r3 = lambda a: a.reshape(b, s, a.shape[-1])
    mixed = _attention(r3(qm), r3(km), r3(vm), r3(dq), r3(dk), r3(dv), adiag, acorner, mask, lam_vecs,
                       g_subln[0][None, :])

    out = _out_ffn(mixed.reshape(b * s, 1024), x2d, w_o[0].astype(_BF16), g_ffn[0][None, :],
                   w_ffn_gate[0].astype(_BF16), w_ffn_up[0].astype(_BF16), w_ffn_down[0].astype(_BF16),
                   g_final[None, :])
    return out.reshape(b, s, d)
```

```python
import math

import numpy as np
import jax
import jax.numpy as jnp
from jax import lax
from jax.experimental import pallas as pl
from jax.experimental.pallas import tpu as pltpu

D_MODEL = 1024
CHUNK = 64
EPS = 1e-6
NEG_INF = -1e30

MLA_HEADS = 8
MLA_NOPE_DIM = 64
MLA_ROPE_DIM = 32
MLA_V_DIM = 64
MLA_Q_RANK = 256
MLA_KV_RANK = 128
ROPE_THETA = 10000.0
ROPE_HALF = MLA_ROPE_DIM // 2

DIFF_HEADS = 4
DIFF_HEAD_DIM = 64
DIFF_V_DIM = 2 * DIFF_HEAD_DIM

REL_BUCKETS = 32
REL_MAX_DIST = 128

FFN_HIDDEN = 2816
LAM_INIT = 0.8 - 0.6 * math.exp(-0.3 * 0)

LOG2E = 1.4426950408889634
MLA_SCALE = (MLA_NOPE_DIM + MLA_ROPE_DIM) ** -0.5
DIFF_SCALE = DIFF_HEAD_DIM ** -0.5

LANES = 128
PROJ_ROWS = 512
ATTN_TILE = 512
FFN_ROWS = 512
FFN_CHUNKS = ((0, 1024), (1024, 2048), (2048, FFN_HIDDEN))
VMEM_LIMIT_BYTES = 56 * 1024 * 1024

MLA_WIDTH = MLA_HEADS * LANES
MLA_V_WIDTH = MLA_HEADS * MLA_V_DIM
DIFF_WIDTH = DIFF_HEADS * LANES

_BF16 = jnp.bfloat16
_F32 = jnp.float32


def _rms(x, g):
    return x * lax.rsqrt(jnp.mean(x * x, axis=-1, keepdims=True) + EPS) * g


def _dot(a, b):
    return jnp.dot(a, b, preferred_element_type=_F32)


def _dot_nt(a, b):
    return lax.dot_general(a, b, (((1,), (1,)), ((), ())), preferred_element_type=_F32)


def _t5_bucket(rel):
    nb = REL_BUCKETS // 2
    max_exact = nb // 2
    ret = (rel > 0).astype(jnp.int32) * nb
    n = jnp.abs(rel)
    nf = jnp.maximum(n, 1).astype(jnp.float32)
    large = max_exact + (jnp.log(nf / max_exact) / math.log(REL_MAX_DIST / max_exact)
                         * (nb - max_exact)).astype(jnp.int32)
    large = jnp.minimum(large, nb - 1)
    return ret + jnp.where(n < max_exact, n, large)


def _bias_kernel(rel_ref, bdiag_ref, bcorner_ref, adiag_ref, acorner_ref, mask_ref):
    t = ATTN_TILE
    key = lax.broadcasted_iota(jnp.int32, (t, t), 0) // CHUNK
    qry = lax.broadcasted_iota(jnp.int32, (t, t), 1) // CHUNK
    mask = jnp.where(key <= qry, 0.0, NEG_INF).astype(_F32)
    mask_ref[...] = mask
    bd = bdiag_ref[...]
    bc = bcorner_ref[...]
    far_bucket = REL_BUCKETS // 2 - 1
    for h in range(DIFF_HEADS):
        far = rel_ref[far_bucket, h]
        vd = jnp.zeros((t, t), _F32)
        vc = jnp.zeros((LANES, LANES), _F32)
        for b in range(REL_BUCKETS):
            val = rel_ref[b, h]
            vd = jnp.where(bd == b, val, vd)
            vc = jnp.where(bc == b, val, vc)
        adiag_ref[h] = (vd - far) * LOG2E + mask
        acorner_ref[h] = (vc - far) * LOG2E


def _bias_tables(rel_bias):
    t = ATTN_TILE
    i = jnp.arange(t)
    bdiag = _t5_bucket(i[:, None] - i[None, :])
    c = jnp.arange(LANES)
    bcorner = _t5_bucket(c[:, None] - c[None, :] - LANES)
    vmem = pl.BlockSpec(memory_space=pltpu.VMEM)
    return pl.pallas_call(
        _bias_kernel,
        out_shape=(jax.ShapeDtypeStruct((DIFF_HEADS, t, t), _F32),
                   jax.ShapeDtypeStruct((DIFF_HEADS, LANES, LANES), _F32),
                   jax.ShapeDtypeStruct((t, t), _F32)),
        in_specs=[pl.BlockSpec(memory_space=pltpu.SMEM), vmem, vmem],
        out_specs=(vmem, vmem, vmem),
        name="bias_tables",
    )(rel_bias, bdiag, bcorner)


def _proj_kernel(x_ref, gattn_ref, wa_ref, wdq_ref, wdv_ref, gq_ref, wq_ref, gkv_ref, wk_ref, wv_ref,
                 cos_ref, sin_ref, cos_t_ref, sin_t_ref,
                 qt_ref, k_ref, vt_ref, dqt_ref, dk_ref, dvt_ref):
    rows = x_ref.shape[0]
    h = _rms(x_ref[...], gattn_ref[...]).astype(_BF16)
    pa = _dot(h, wa_ref[...])
    dk_ref[...] = pa[:, 512:1024].astype(_BF16)
    dqt_ref[0] = (_dot_nt(wdq_ref[...], h) * (DIFF_SCALE * LOG2E)).astype(_BF16)
    dvt_ref[0] = _dot_nt(wdv_ref[...], h).astype(_BF16)

    cq = _rms(pa[:, 0:256], gq_ref[...]).astype(_BF16)
    ckv = _rms(pa[:, 256:384], gkv_ref[...]).astype(_BF16)

    qt = _dot_nt(wq_ref[...], cq)
    cos_t = cos_t_ref[...]
    sin_t = sin_t_ref[...]
    parts = []
    for hd in range(MLA_HEADS):
        b0 = hd * LANES
        r0 = b0 + MLA_NOPE_DIM
        x1 = qt[r0:r0 + ROPE_HALF]
        x2 = qt[r0 + ROPE_HALF:r0 + MLA_ROPE_DIM]
        parts += [qt[b0:r0], x1 * cos_t - x2 * sin_t, x2 * cos_t + x1 * sin_t,
                  qt[r0 + MLA_ROPE_DIM:b0 + LANES]]
    qt_ref[0] = (jnp.concatenate(parts, axis=0) * (MLA_SCALE * LOG2E)).astype(_BF16)

    cos = cos_ref[...]
    sin = sin_ref[...]
    lane = lax.broadcasted_iota(jnp.int32, (rows, LANES), 1)
    first = (lane >= MLA_NOPE_DIM) & (lane < MLA_NOPE_DIM + ROPE_HALF)
    kr = pa[:, 384:512]
    kr = kr * cos + jnp.where(first, pltpu.roll(kr, LANES - ROPE_HALF, 1), pltpu.roll(kr, ROPE_HALF, 1)) * sin
    kk = _dot(ckv, wk_ref[...])
    k_ref[...] = (kk + jnp.concatenate([kr] * MLA_HEADS, axis=1)).astype(_BF16)
    vt_ref[0] = _dot_nt(wv_ref[...], ckv).astype(_BF16)


def _projections(x2d, g_attn, w_a, w_dq_t, w_dv_t, g_q, w_q_t, g_kv, w_k_p, w_v_t,
                 cos_l, sin_l, cos_t, sin_t, batch, seq):
    n = x2d.shape[0]
    r = PROJ_ROWS
    tps = seq // r

    def rows(width):
        return pl.BlockSpec((r, width), lambda i: (i, 0))

    def cols(height):
        return pl.BlockSpec((1, height, r), lambda i: (i // tps, 0, i % tps))

    def whole(a):
        return pl.BlockSpec(a.shape, lambda i: (0, 0))

    pos_rows = pl.BlockSpec((r, LANES), lambda i: (i % tps, 0))
    pos_cols = pl.BlockSpec((ROPE_HALF, r), lambda i: (0, i % tps))
    tok = lambda w: jax.ShapeDtypeStruct((n, w), _BF16)
    feat = lambda hgt: jax.ShapeDtypeStruct((batch, hgt, seq), _BF16)
    return pl.pallas_call(
        _proj_kernel,
        grid=(n // r,),
        in_specs=[rows(D_MODEL), whole(g_attn), whole(w_a), whole(w_dq_t), whole(w_dv_t), whole(g_q),
                  whole(w_q_t), whole(g_kv), whole(w_k_p), whole(w_v_t), pos_rows, pos_rows, pos_cols, pos_cols],
        out_specs=(cols(MLA_WIDTH), rows(MLA_WIDTH), cols(MLA_V_WIDTH), cols(DIFF_WIDTH), rows(DIFF_WIDTH),
                   cols(DIFF_WIDTH)),
        out_shape=(feat(MLA_WIDTH), tok(MLA_WIDTH), feat(MLA_V_WIDTH), feat(DIFF_WIDTH), tok(DIFF_WIDTH),
                   feat(DIFF_WIDTH)),
        compiler_params=pltpu.CompilerParams(dimension_semantics=("parallel",),
                                             vmem_limit_bytes=VMEM_LIMIT_BYTES),
        name="projections",
    )(x2d, g_attn, w_a, w_dq_t, w_dv_t, g_q, w_q_t, g_kv, w_k_p, w_v_t, cos_l, sin_l, cos_t, sin_t)


def _online_softmax_step(s_t, v_t, m_ref, l_ref, acc_ref, h, acc_rows):
    m_prev = m_ref[h]
    m_new = jnp.maximum(m_prev, jnp.max(s_t, axis=0, keepdims=True))
    alpha = jnp.exp2(m_prev - m_new)
    p_t = jnp.exp2(s_t - m_new)
    l_ref[h] = alpha * l_ref[h] + jnp.sum(p_t, axis=0, keepdims=True)
    acc_ref[acc_rows] = alpha * acc_ref[acc_rows] + _dot(v_t, p_t.astype(_BF16))
    m_ref[h] = m_new


def _attn_kernel(qi_tab, ki_tab,
                 qt_ref, k_ref, vt_ref, dqt_ref, dk_ref, dvt_ref,
                 adiag_ref, acorner_ref, mask_ref, lam_ref, gsub_ref,
                 out_ref,
                 m_a, l_a, acc_a, m_d, l_d, acc_d):
    t = ATTN_TILE
    step = pl.program_id(1)
    qi = qi_tab[step]
    ki = ki_tab[step]

    @pl.when(ki == 0)
    def _init():
        m_a[...] = jnp.full(m_a.shape, -jnp.inf, _F32)
        l_a[...] = jnp.zeros(l_a.shape, _F32)
        acc_a[...] = jnp.zeros(acc_a.shape, _F32)
        m_d[...] = jnp.full(m_d.shape, -jnp.inf, _F32)
        l_d[...] = jnp.zeros(l_d.shape, _F32)
        acc_d[...] = jnp.zeros(acc_d.shape, _F32)

    def add_corner(s_t, corner):
        bot = s_t[t - LANES:]
        bot = jnp.concatenate([bot[:, :LANES] + corner, bot[:, LANES:t],
                               bot[:, t:t + LANES] + corner, bot[:, t + LANES:]], axis=1)
        return jnp.concatenate([s_t[:t - LANES], bot], axis=0)

    def tile_step(kind):
        for h in range(MLA_HEADS):
            blk = slice(h * LANES, (h + 1) * LANES)
            s_t = _dot(k_ref[0, :, blk], qt_ref[0, blk, :])
            if kind == "diag":
                s_t = s_t + mask_ref[...]
            vrows = slice(h * MLA_V_DIM, (h + 1) * MLA_V_DIM)
            _online_softmax_step(s_t, vt_ref[0, vrows, :], m_a, l_a, acc_a, h, vrows)

        feat = lax.broadcasted_iota(jnp.int32, (LANES, t), 0)
        for h in range(DIFF_HEADS):
            blk = slice(h * LANES, (h + 1) * LANES)
            q = dqt_ref[0, blk, :]
            zero = jnp.zeros_like(q)
            q12 = jnp.concatenate([jnp.where(feat < DIFF_HEAD_DIM, q, zero),
                                   jnp.where(feat >= DIFF_HEAD_DIM, q, zero)], axis=1)
            s_t = _dot(dk_ref[0, :, blk], q12)
            if kind == "diag":
                a = adiag_ref[h]
                s_t = s_t + jnp.concatenate([a, a], axis=1)
            elif kind == "sub":
                s_t = add_corner(s_t, acorner_ref[h])
            _online_softmax_step(s_t, dvt_ref[0, blk, :], m_d, l_d, acc_d, h, h)

    @pl.when(ki < qi - 1)
    def _far():
        tile_step("far")

    @pl.when(ki == qi - 1)
    def _sub():
        tile_step("sub")

    @pl.when(ki == qi)
    def _diag():
        tile_step("diag")
        feat = lax.broadcasted_iota(jnp.int32, (LANES, t), 0)
        for j in range(MLA_HEADS // 2):
            inv = jnp.where(feat < MLA_V_DIM, 1.0 / l_a[2 * j], 1.0 / l_a[2 * j + 1])
            o_t = acc_a[j * LANES:(j + 1) * LANES, :] * inv
            out_ref[0, :, j * LANES:(j + 1) * LANES] = o_t.T.astype(_BF16)
        lam = (jnp.exp(jnp.sum(lam_ref[0:1, :] * lam_ref[1:2, :], axis=-1, keepdims=True))
               - jnp.exp(jnp.sum(lam_ref[2:3, :] * lam_ref[3:4, :], axis=-1, keepdims=True))
               + LAM_INIT)
        base = MLA_V_WIDTH
        for h in range(DIFF_HEADS):
            acc = acc_d[h]
            inv = 1.0 / l_d[h]
            o_t = acc[:, :t] * inv[:, :t] - lam * (acc[:, t:] * inv[:, t:])
            y = _rms(o_t.T, gsub_ref[...]) * (1.0 - LAM_INIT)
            out_ref[0, :, base + h * LANES:base + (h + 1) * LANES] = y.astype(_BF16)


def _attention(qt, k, vt, dqt, dk, dvt, adiag, acorner, mask, lam_vecs, g_sub):
    b, s, _ = k.shape
    t = ATTN_TILE
    nq = s // t
    pairs = [(qi, ki) for qi in range(nq) for ki in range(qi + 1)]
    qi_tab = jnp.asarray(np.array([p[0] for p in pairs], np.int32))
    ki_tab = jnp.asarray(np.array([p[1] for p in pairs], np.int32))

    def tok(width, tab):
        return pl.BlockSpec((1, t, width), lambda bi, p, qt_, kt_: (bi, (qt_, kt_)[tab][p], 0))

    def feat(height, tab):
        return pl.BlockSpec((1, height, t), lambda bi, p, qt_, kt_: (bi, 0, (qt_, kt_)[tab][p]))

    def const(a):
        nd = a.ndim
        return pl.BlockSpec(a.shape, lambda bi, p, qt_, kt_: (0,) * nd)

    grid_spec = pltpu.PrefetchScalarGridSpec(
        num_scalar_prefetch=2,
        grid=(b, len(pairs)),
        in_specs=[feat(MLA_WIDTH, 0), tok(MLA_WIDTH, 1), feat(MLA_V_WIDTH, 1),
                  feat(DIFF_WIDTH, 0), tok(DIFF_WIDTH, 1), feat(DIFF_WIDTH, 1),
                  const(adiag), const(acorner), const(mask), const(lam_vecs), const(g_sub)],
        out_specs=tok(MLA_V_WIDTH + DIFF_WIDTH, 0),
        scratch_shapes=[
            pltpu.VMEM((MLA_HEADS, 1, t), _F32), pltpu.VMEM((MLA_HEADS, 1, t), _F32),
            pltpu.VMEM((MLA_V_WIDTH, t), _F32),
            pltpu.VMEM((DIFF_HEADS, 1, 2 * t), _F32), pltpu.VMEM((DIFF_HEADS, 1, 2 * t), _F32),
            pltpu.VMEM((DIFF_HEADS, LANES, 2 * t), _F32),
        ],
    )
    return pl.pallas_call(
        _attn_kernel,
        grid_spec=grid_spec,
        out_shape=jax.ShapeDtypeStruct((b, s, MLA_V_WIDTH + DIFF_WIDTH), _BF16),
        compiler_params=pltpu.CompilerParams(dimension_semantics=("parallel", "arbitrary"),
                                             vmem_limit_bytes=VMEM_LIMIT_BYTES),
        name="attention",
    )(qi_tab, ki_tab, qt, k, vt, dqt, dk, dvt, adiag, acorner, mask, lam_vecs, g_sub)


def _ffn_kernel(mixed_ref, x_ref, wo_ref, gffn_ref, wg_ref, wu_ref, wd_ref, gfin_ref, out_ref):
    x1 = x_ref[...] + _dot(mixed_ref[...], wo_ref[...])
    h = _rms(x1, gffn_ref[...]).astype(_BF16)
    y = jnp.zeros_like(x1)
    for c0, c1 in FFN_CHUNKS:
        g = _dot(h, wg_ref[:, c0:c1])
        u = _dot(h, wu_ref[:, c0:c1])
        act = (g * (1.0 / (1.0 + jnp.exp(-g))) * u).astype(_BF16)
        y = y + _dot(act, wd_ref[c0:c1, :])
    out_ref[...] = _rms(x1 + y, gfin_ref[...])


def _out_ffn(mixed2d, x2d, w_o, g_ffn, w_g, w_u, w_d, g_fin):
    n = x2d.shape[0]
    r = FFN_ROWS

    def rows(width):
        return pl.BlockSpec((r, width), lambda i: (i, 0))

    def resident(a):
        return pl.BlockSpec(a.shape, lambda i: (0, 0), pipeline_mode=pl.Buffered(1))

    return pl.pallas_call(
        _ffn_kernel,
        grid=(n // r,),
        in_specs=[rows(mixed2d.shape[1]), rows(D_MODEL), resident(w_o), resident(g_ffn), resident(w_g),
                  resident(w_u), resident(w_d), resident(g_fin)],
        out_specs=rows(D_MODEL),
        out_shape=jax.ShapeDtypeStruct((n, D_MODEL), _F32),
        compiler_params=pltpu.CompilerParams(dimension_semantics=("parallel",),
                                             vmem_limit_bytes=VMEM_LIMIT_BYTES),
        name="out_ffn",
    )(mixed2d, x2d, w_o, g_ffn, w_g, w_u, w_d, g_fin)


def _rope_tables(seq):
    inv_freq = ROPE_THETA ** (-jnp.arange(ROPE_HALF, dtype=jnp.float32) / ROPE_HALF)
    ang = jnp.arange(seq).astype(jnp.float32)[:, None] * inv_freq[None, :]
    cos = jnp.cos(ang)
    sin = jnp.sin(ang)
    one = jnp.ones((seq, MLA_NOPE_DIM), _F32)
    zero = jnp.zeros((seq, MLA_NOPE_DIM), _F32)
    pad = LANES - MLA_NOPE_DIM - MLA_ROPE_DIM
    cos_l = jnp.concatenate([one, cos, cos, one[:, :pad]], axis=1)
    sin_l = jnp.concatenate([zero, -sin, sin, zero[:, :pad]], axis=1)
    return cos_l, sin_l, cos.T, sin.T


def kernel(x, w_in, g_attn, g_q_a, w_q_b, g_kv_a, w_kv_b, lam_q1, lam_k1, lam_q2, lam_k2, g_subln,
           rel_bias, w_o, g_ffn, w_ffn_gate, w_ffn_up, w_ffn_down, g_final):
    b, s, d = x.shape
    assert d == D_MODEL and s % ATTN_TILE == 0 and s % PROJ_ROWS == 0 and (b * s) % FFN_ROWS == 0
    assert w_in.shape[0] == 1, "single layer"

    w = w_in[0]
    zc = lambda n: jnp.zeros((D_MODEL, n), w.dtype)
    pad = LANES - MLA_NOPE_DIM - MLA_ROPE_DIM
    w_a = jnp.concatenate([w[:, 0:384], zc(MLA_NOPE_DIM), w[:, 384:416], zc(pad), w[:, 928:1440]],
                          axis=1).astype(_BF16)
    w_dq_t = w[:, 416:928].T.astype(_BF16)
    w_dv_t = w[:, 1440:1952].T.astype(_BF16)
    wq = w_q_b[0].reshape(MLA_Q_RANK, MLA_HEADS, MLA_NOPE_DIM + MLA_ROPE_DIM)
    w_q_t = jnp.concatenate([wq, jnp.zeros((MLA_Q_RANK, MLA_HEADS, pad), wq.dtype)],
                            axis=2).reshape(MLA_Q_RANK, MLA_WIDTH).T.astype(_BF16)
    wkv = w_kv_b[0].reshape(MLA_KV_RANK, MLA_HEADS, MLA_NOPE_DIM + MLA_V_DIM)
    w_k_p = jnp.concatenate([wkv[:, :, :MLA_NOPE_DIM],
                             jnp.zeros((MLA_KV_RANK, MLA_HEADS, LANES - MLA_NOPE_DIM), wkv.dtype)],
                            axis=2).reshape(MLA_KV_RANK, MLA_WIDTH).astype(_BF16)
    w_v_t = wkv[:, :, MLA_NOPE_DIM:].reshape(MLA_KV_RANK, MLA_V_WIDTH).T.astype(_BF16)

    cos_l, sin_l, cos_t, sin_t = _rope_tables(s)
    x2d = x.reshape(b * s, d)
    qt, k, vt, dqt, dk, dvt = _projections(
        x2d, g_attn[0][None, :], w_a, w_dq_t, w_dv_t, g_q_a[0][None, :], w_q_t, g_kv_a[0][None, :],
        w_k_p, w_v_t, cos_l, sin_l, cos_t, sin_t, b, s)

    adiag, acorner, mask = _bias_tables(rel_bias)
    lam_vecs = jnp.stack([lam_q1[0], lam_k1[0], lam_q2[0], lam_k2[0]]).astype(_F32)
    mixed = _attention(qt, k.reshape(b, s, MLA_WIDTH), vt, dqt, dk.reshape(b, s, DIFF_WIDTH), dvt,
                       adiag, acorner, mask, lam_vecs, g_subln[0][None, :])

    out = _out_ffn(mixed.reshape(b * s, MLA_V_WIDTH + DIFF_WIDTH), x2d, w_o[0].astype(_BF16),
                   g_ffn[0][None, :], w_ffn_gate[0].astype(_BF16), w_ffn_up[0].astype(_BF16),
                   w_ffn_down[0].astype(_BF16), g_final[None, :])
    return out.reshape(b, s, d)
```

```python
import math

import numpy as np
import jax
import jax.numpy as jnp
from jax import lax
from jax.experimental import pallas as pl
from jax.experimental.pallas import tpu as pltpu

D_MODEL = 1024
CHUNK = 64
EPS = 1e-6
NEG_INF = -1e30

MLA_HEADS = 8
MLA_NOPE_DIM = 64
MLA_ROPE_DIM = 32
MLA_V_DIM = 64
MLA_Q_RANK = 256
MLA_KV_RANK = 128
ROPE_THETA = 10000.0
ROPE_HALF = MLA_ROPE_DIM // 2

DIFF_HEADS = 4
DIFF_HEAD_DIM = 64
DIFF_V_DIM = 2 * DIFF_HEAD_DIM

REL_BUCKETS = 32
REL_MAX_DIST = 128

FFN_HIDDEN = 2816
LAM_INIT = 0.8 - 0.6 * math.exp(-0.3 * 0)

LOG2E = 1.4426950408889634
MLA_SCALE = (MLA_NOPE_DIM + MLA_ROPE_DIM) ** -0.5
DIFF_SCALE = DIFF_HEAD_DIM ** -0.5

LANES = 128
ONES_ROWS = 16
PROJ_ROWS = 512
ATTN_TILE = 512
FFN_ROWS = 512
FFN_CHUNKS = ((0, 1024), (1024, 2048), (2048, FFN_HIDDEN))
VMEM_LIMIT_BYTES = 56 * 1024 * 1024

MLA_WIDTH = MLA_HEADS * LANES
MLA_V_WIDTH = MLA_HEADS * MLA_V_DIM
DIFF_WIDTH = DIFF_HEADS * LANES

_BF16 = jnp.bfloat16
_F32 = jnp.float32


def _rms(x, g):
    return x * lax.rsqrt(jnp.mean(x * x, axis=-1, keepdims=True) + EPS) * g


def _dot(a, b):
    return jnp.dot(a, b, preferred_element_type=_F32)


def _dot_nt(a, b):
    return lax.dot_general(a, b, (((1,), (1,)), ((), ())), preferred_element_type=_F32)


def _t5_bucket(rel):
    nb = REL_BUCKETS // 2
    max_exact = nb // 2
    ret = (rel > 0).astype(jnp.int32) * nb
    n = jnp.abs(rel)
    nf = jnp.maximum(n, 1).astype(jnp.float32)
    large = max_exact + (jnp.log(nf / max_exact) / math.log(REL_MAX_DIST / max_exact)
                         * (nb - max_exact)).astype(jnp.int32)
    large = jnp.minimum(large, nb - 1)
    return ret + jnp.where(n < max_exact, n, large)


def _bias_kernel(rel_ref, bdiag_ref, bcorner_ref, adiag_ref, acorner_ref, mask_ref):
    t = ATTN_TILE
    key = lax.broadcasted_iota(jnp.int32, (t, t), 0) // CHUNK
    qry = lax.broadcasted_iota(jnp.int32, (t, t), 1) // CHUNK
    mask = jnp.where(key <= qry, 0.0, NEG_INF).astype(_F32)
    mask_ref[...] = mask
    bd = bdiag_ref[...]
    bc = bcorner_ref[...]
    far_bucket = REL_BUCKETS // 2 - 1
    for h in range(DIFF_HEADS):
        far = rel_ref[far_bucket, h]
        vd = jnp.zeros((t, t), _F32)
        vc = jnp.zeros((LANES, LANES), _F32)
        for b in range(REL_BUCKETS):
            val = rel_ref[b, h]
            vd = jnp.where(bd == b, val, vd)
            vc = jnp.where(bc == b, val, vc)
        adiag_ref[h] = (vd - far) * LOG2E + mask
        acorner_ref[h] = (vc - far) * LOG2E


def _bias_tables(rel_bias):
    t = ATTN_TILE
    i = jnp.arange(t)
    bdiag = _t5_bucket(i[:, None] - i[None, :])
    c = jnp.arange(LANES)
    bcorner = _t5_bucket(c[:, None] - c[None, :] - LANES)
    vmem = pl.BlockSpec(memory_space=pltpu.VMEM)
    return pl.pallas_call(
        _bias_kernel,
        out_shape=(jax.ShapeDtypeStruct((DIFF_HEADS, t, t), _F32),
                   jax.ShapeDtypeStruct((DIFF_HEADS, LANES, LANES), _F32),
                   jax.ShapeDtypeStruct((t, t), _F32)),
        in_specs=[pl.BlockSpec(memory_space=pltpu.SMEM), vmem, vmem],
        out_specs=(vmem, vmem, vmem),
        name="bias_tables",
    )(rel_bias, bdiag, bcorner)


def _proj_kernel(x_ref, gattn_ref, wa_ref, wdq_ref, wdv_ref, gq_ref, wq_ref, gkv_ref, wk_ref, wv_ref,
                 cos_ref, sin_ref, cos_t_ref, sin_t_ref,
                 qt_ref, k_ref, vt_ref, dqt_ref, dk_ref, dvt_ref):
    rows = x_ref.shape[0]
    h = _rms(x_ref[...], gattn_ref[...]).astype(_BF16)
    pa = _dot(h, wa_ref[...])
    dk_ref[...] = pa[:, 512:1024].astype(_BF16)
    dqt_ref[0] = (_dot_nt(wdq_ref[...], h) * (DIFF_SCALE * LOG2E)).astype(_BF16)
    dvt_ref[0] = _dot_nt(wdv_ref[...], h).astype(_BF16)

    cq = _rms(pa[:, 0:256], gq_ref[...]).astype(_BF16)
    ckv = _rms(pa[:, 256:384], gkv_ref[...]).astype(_BF16)

    qt = _dot_nt(wq_ref[...], cq)
    cos_t = cos_t_ref[...]
    sin_t = sin_t_ref[...]
    parts = []
    for hd in range(MLA_HEADS):
        b0 = hd * LANES
        r0 = b0 + MLA_NOPE_DIM
        x1 = qt[r0:r0 + ROPE_HALF]
        x2 = qt[r0 + ROPE_HALF:r0 + MLA_ROPE_DIM]
        parts += [qt[b0:r0], x1 * cos_t - x2 * sin_t, x2 * cos_t + x1 * sin_t,
                  qt[r0 + MLA_ROPE_DIM:b0 + LANES]]
    qt_ref[0] = (jnp.concatenate(parts, axis=0) * (MLA_SCALE * LOG2E)).astype(_BF16)

    cos = cos_ref[...]
    sin = sin_ref[...]
    lane = lax.broadcasted_iota(jnp.int32, (rows, LANES), 1)
    first = (lane >= MLA_NOPE_DIM) & (lane < MLA_NOPE_DIM + ROPE_HALF)
    kr = pa[:, 384:512]
    kr = kr * cos + jnp.where(first, pltpu.roll(kr, LANES - ROPE_HALF, 1), pltpu.roll(kr, ROPE_HALF, 1)) * sin
    kk = _dot(ckv, wk_ref[...])
    k_ref[...] = (kk + jnp.concatenate([kr] * MLA_HEADS, axis=1)).astype(_BF16)
    vt_ref[0] = _dot_nt(wv_ref[...], ckv).astype(_BF16)


def _projections(x2d, g_attn, w_a, w_dq_t, w_dv_t, g_q, w_q_t, g_kv, w_k_p, w_v_t,
                 cos_l, sin_l, cos_t, sin_t, batch, seq):
    n = x2d.shape[0]
    r = PROJ_ROWS
    tps = seq // r

    def rows(width):
        return pl.BlockSpec((r, width), lambda i: (i, 0))

    def cols(height):
        return pl.BlockSpec((1, height, r), lambda i: (i // tps, 0, i % tps))

    def whole(a):
        return pl.BlockSpec(a.shape, lambda i: (0, 0))

    pos_rows = pl.BlockSpec((r, LANES), lambda i: (i % tps, 0))
    pos_cols = pl.BlockSpec((ROPE_HALF, r), lambda i: (0, i % tps))
    tok = lambda w: jax.ShapeDtypeStruct((n, w), _BF16)
    feat = lambda hgt: jax.ShapeDtypeStruct((batch, hgt, seq), _BF16)
    return pl.pallas_call(
        _proj_kernel,
        grid=(n // r,),
        in_specs=[rows(D_MODEL), whole(g_attn), whole(w_a), whole(w_dq_t), whole(w_dv_t), whole(g_q),
                  whole(w_q_t), whole(g_kv), whole(w_k_p), whole(w_v_t), pos_rows, pos_rows, pos_cols, pos_cols],
        out_specs=(cols(MLA_WIDTH), rows(MLA_WIDTH), cols(MLA_V_WIDTH), cols(DIFF_WIDTH), rows(DIFF_WIDTH),
                   cols(DIFF_WIDTH)),
        out_shape=(feat(MLA_WIDTH), tok(MLA_WIDTH), feat(MLA_V_WIDTH), feat(DIFF_WIDTH), tok(DIFF_WIDTH),
                   feat(DIFF_WIDTH)),
        compiler_params=pltpu.CompilerParams(dimension_semantics=("parallel",),
                                             vmem_limit_bytes=VMEM_LIMIT_BYTES),
        name="projections",
    )(x2d, g_attn, w_a, w_dq_t, w_dv_t, g_q, w_q_t, g_kv, w_k_p, w_v_t, cos_l, sin_l, cos_t, sin_t)


def _softmax_pv_step(s_t, v_t, m_ref, acc_ref, h):
    m_prev = m_ref[h]
    m_new = jnp.maximum(m_prev, jnp.max(s_t, axis=0, keepdims=True))
    alpha = jnp.exp2(m_prev - m_new)
    p_t = jnp.exp2(s_t - m_new).astype(_BF16)
    ones = (lax.broadcasted_iota(jnp.int32, (ONES_ROWS, v_t.shape[1]), 0) == 0).astype(_BF16)
    v_aug = jnp.concatenate([v_t, ones], axis=0)
    acc_ref[h] = alpha * acc_ref[h] + _dot(v_aug, p_t)
    m_ref[h] = m_new


def _attn_kernel(qi_tab, ki_tab,
                 qt_ref, k_ref, vt_ref, dqt_ref, dk_ref, dvt_ref,
                 adiag_ref, acorner_ref, mask_ref, lam_ref, gsub_ref,
                 out_ref,
                 m_a, acc_a, m_d, acc_d):
    t = ATTN_TILE
    step = pl.program_id(1)
    qi = qi_tab[step]
    ki = ki_tab[step]

    @pl.when(ki == 0)
    def _init():
        m_a[...] = jnp.full(m_a.shape, -jnp.inf, _F32)
        acc_a[...] = jnp.zeros(acc_a.shape, _F32)
        m_d[...] = jnp.full(m_d.shape, -jnp.inf, _F32)
        acc_d[...] = jnp.zeros(acc_d.shape, _F32)

    def add_corner(s_t, corner):
        bot = s_t[t - LANES:]
        bot = jnp.concatenate([bot[:, :LANES] + corner, bot[:, LANES:t],
                               bot[:, t:t + LANES] + corner, bot[:, t + LANES:]], axis=1)
        return jnp.concatenate([s_t[:t - LANES], bot], axis=0)

    def scores(unit):
        if unit < MLA_HEADS:
            blk = slice(unit * LANES, (unit + 1) * LANES)
            return _dot(k_ref[0, :, blk], qt_ref[0, blk, :])
        h = unit - MLA_HEADS
        blk = slice(h * LANES, (h + 1) * LANES)
        feat = lax.broadcasted_iota(jnp.int32, (LANES, t), 0)
        q = dqt_ref[0, blk, :]
        zero = jnp.zeros_like(q)
        q12 = jnp.concatenate([jnp.where(feat < DIFF_HEAD_DIM, q, zero),
                               jnp.where(feat >= DIFF_HEAD_DIM, q, zero)], axis=1)
        return _dot(dk_ref[0, :, blk], q12)

    def softmax_pv(unit, s_t, kind):
        if unit < MLA_HEADS:
            if kind == "diag":
                s_t = s_t + mask_ref[...]
            vrows = slice(unit * MLA_V_DIM, (unit + 1) * MLA_V_DIM)
            _softmax_pv_step(s_t, vt_ref[0, vrows, :], m_a, acc_a, unit)
        else:
            h = unit - MLA_HEADS
            if kind == "diag":
                a = adiag_ref[h]
                s_t = s_t + jnp.concatenate([a, a], axis=1)
            elif kind == "sub":
                s_t = add_corner(s_t, acorner_ref[h])
            _softmax_pv_step(s_t, dvt_ref[0, h * LANES:(h + 1) * LANES, :], m_d, acc_d, h)

    def tile_step(kind):
        units = MLA_HEADS + DIFF_HEADS
        s_next = scores(0)
        for unit in range(units):
            s_cur = s_next
            if unit + 1 < units:
                s_next = scores(unit + 1)
            softmax_pv(unit, s_cur, kind)

    @pl.when(ki < qi - 1)
    def _far():
        tile_step("far")

    @pl.when(ki == qi - 1)
    def _sub():
        tile_step("sub")

    @pl.when(ki == qi)
    def _diag():
        tile_step("diag")
        for j in range(MLA_HEADS // 2):
            a0 = acc_a[2 * j]
            a1 = acc_a[2 * j + 1]
            o_t = jnp.concatenate([a0[:MLA_V_DIM] * (1.0 / a0[MLA_V_DIM:MLA_V_DIM + 1]),
                                   a1[:MLA_V_DIM] * (1.0 / a1[MLA_V_DIM:MLA_V_DIM + 1])], axis=0)
            out_ref[0, :, j * LANES:(j + 1) * LANES] = o_t.T.astype(_BF16)
        lam = (jnp.exp(jnp.sum(lam_ref[0:1, :] * lam_ref[1:2, :], axis=-1, keepdims=True))
               - jnp.exp(jnp.sum(lam_ref[2:3, :] * lam_ref[3:4, :], axis=-1, keepdims=True))
               + LAM_INIT)
        base = MLA_V_WIDTH
        for h in range(DIFF_HEADS):
            acc = acc_d[h]
            inv = 1.0 / acc[DIFF_V_DIM:DIFF_V_DIM + 1]
            o_t = acc[:DIFF_V_DIM, :t] * inv[:, :t] - lam * (acc[:DIFF_V_DIM, t:] * inv[:, t:])
            y = _rms(o_t.T, gsub_ref[...]) * (1.0 - LAM_INIT)
            out_ref[0, :, base + h * LANES:base + (h + 1) * LANES] = y.astype(_BF16)


def _attention(qt, k, vt, dqt, dk, dvt, adiag, acorner, mask, lam_vecs, g_sub):
    b, s, _ = k.shape
    t = ATTN_TILE
    nq = s // t
    pairs = [(qi, ki) for qi in range(nq) for ki in range(qi + 1)]
    qi_tab = jnp.asarray(np.array([p[0] for p in pairs], np.int32))
    ki_tab = jnp.asarray(np.array([p[1] for p in pairs], np.int32))

    def tok(width, tab):
        return pl.BlockSpec((1, t, width), lambda bi, p, qt_, kt_: (bi, (qt_, kt_)[tab][p], 0))

    def feat(height, tab):
        return pl.BlockSpec((1, height, t), lambda bi, p, qt_, kt_: (bi, 0, (qt_, kt_)[tab][p]))

    def const(a):
        nd = a.ndim
        return pl.BlockSpec(a.shape, lambda bi, p, qt_, kt_: (0,) * nd)

    grid_spec = pltpu.PrefetchScalarGridSpec(
        num_scalar_prefetch=2,
        grid=(b, len(pairs)),
        in_specs=[feat(MLA_WIDTH, 0), tok(MLA_WIDTH, 1), feat(MLA_V_WIDTH, 1),
                  feat(DIFF_WIDTH, 0), tok(DIFF_WIDTH, 1), feat(DIFF_WIDTH, 1),
                  const(adiag), const(acorner), const(mask), const(lam_vecs), const(g_sub)],
        out_specs=tok(MLA_V_WIDTH + DIFF_WIDTH, 0),
        scratch_shapes=[
            pltpu.VMEM((MLA_HEADS, 1, t), _F32),
            pltpu.VMEM((MLA_HEADS, MLA_V_DIM + ONES_ROWS, t), _F32),
            pltpu.VMEM((DIFF_HEADS, 1, 2 * t), _F32),
            pltpu.VMEM((DIFF_HEADS, DIFF_V_DIM + ONES_ROWS, 2 * t), _F32),
        ],
    )
    return pl.pallas_call(
        _attn_kernel,
        grid_spec=grid_spec,
        out_shape=jax.ShapeDtypeStruct((b, s, MLA_V_WIDTH + DIFF_WIDTH), _BF16),
        compiler_params=pltpu.CompilerParams(dimension_semantics=("parallel", "arbitrary"),
                                             vmem_limit_bytes=VMEM_LIMIT_BYTES),
        name="attention",
    )(qi_tab, ki_tab, qt, k, vt, dqt, dk, dvt, adiag, acorner, mask, lam_vecs, g_sub)


def _ffn_kernel(mixed_ref, x_ref, wo_ref, gffn_ref, wg_ref, wu_ref, wd_ref, gfin_ref, out_ref):
    x1 = x_ref[...] + _dot(mixed_ref[...], wo_ref[...])
    h = _rms(x1, gffn_ref[...]).astype(_BF16)
    y = jnp.zeros_like(x1)
    for c0, c1 in FFN_CHUNKS:
        g = _dot(h, wg_ref[:, c0:c1])
        u = _dot(h, wu_ref[:, c0:c1])
        act = (g * (1.0 / (1.0 + jnp.exp(-g))) * u).astype(_BF16)
        y = y + _dot(act, wd_ref[c0:c1, :])
    out_ref[...] = _rms(x1 + y, gfin_ref[...])


def _out_ffn(mixed2d, x2d, w_o, g_ffn, w_g, w_u, w_d, g_fin):
    n = x2d.shape[0]
    r = FFN_ROWS

    def rows(width):
        return pl.BlockSpec((r, width), lambda i: (i, 0))

    def resident(a):
        return pl.BlockSpec(a.shape, lambda i: (0, 0), pipeline_mode=pl.Buffered(1))

    return pl.pallas_call(
        _ffn_kernel,
        grid=(n // r,),
        in_specs=[rows(mixed2d.shape[1]), rows(D_MODEL), resident(w_o), resident(g_ffn), resident(w_g),
                  resident(w_u), resident(w_d), resident(g_fin)],
        out_specs=rows(D_MODEL),
        out_shape=jax.ShapeDtypeStruct((n, D_MODEL), _F32),
        compiler_params=pltpu.CompilerParams(dimension_semantics=("parallel",),
                                             vmem_limit_bytes=VMEM_LIMIT_BYTES),
        name="out_ffn",
    )(mixed2d, x2d, w_o, g_ffn, w_g, w_u, w_d, g_fin)


def _rope_tables(seq):
    inv_freq = ROPE_THETA ** (-jnp.arange(ROPE_HALF, dtype=jnp.float32) / ROPE_HALF)
    ang = jnp.arange(seq).astype(jnp.float32)[:, None] * inv_freq[None, :]
    cos = jnp.cos(ang)
    sin = jnp.sin(ang)
    one = jnp.ones((seq, MLA_NOPE_DIM), _F32)
    zero = jnp.zeros((seq, MLA_NOPE_DIM), _F32)
    pad = LANES - MLA_NOPE_DIM - MLA_ROPE_DIM
    cos_l = jnp.concatenate([one, cos, cos, one[:, :pad]], axis=1)
    sin_l = jnp.concatenate([zero, -sin, sin, zero[:, :pad]], axis=1)
    return cos_l, sin_l, cos.T, sin.T


def kernel(x, w_in, g_attn, g_q_a, w_q_b, g_kv_a, w_kv_b, lam_q1, lam_k1, lam_q2, lam_k2, g_subln,
           rel_bias, w_o, g_ffn, w_ffn_gate, w_ffn_up, w_ffn_down, g_final):
    b, s, d = x.shape
    assert d == D_MODEL and s % ATTN_TILE == 0 and s % PROJ_ROWS == 0 and (b * s) % FFN_ROWS == 0
    assert w_in.shape[0] == 1, "single layer"

    w = w_in[0]
    zc = lambda n: jnp.zeros((D_MODEL, n), w.dtype)
    pad = LANES - MLA_NOPE_DIM - MLA_ROPE_DIM
    w_a = jnp.concatenate([w[:, 0:384], zc(MLA_NOPE_DIM), w[:, 384:416], zc(pad), w[:, 928:1440]],
                          axis=1).astype(_BF16)
    w_dq_t = w[:, 416:928].T.astype(_BF16)
    w_dv_t = w[:, 1440:1952].T.astype(_BF16)
    wq = w_q_b[0].reshape(MLA_Q_RANK, MLA_HEADS, MLA_NOPE_DIM + MLA_ROPE_DIM)
    w_q_t = jnp.concatenate([wq, jnp.zeros((MLA_Q_RANK, MLA_HEADS, pad), wq.dtype)],
                            axis=2).reshape(MLA_Q_RANK, MLA_WIDTH).T.astype(_BF16)
    wkv = w_kv_b[0].reshape(MLA_KV_RANK, MLA_HEADS, MLA_NOPE_DIM + MLA_V_DIM)
    w_k_p = jnp.concatenate([wkv[:, :, :MLA_NOPE_DIM],
                             jnp.zeros((MLA_KV_RANK, MLA_HEADS, LANES - MLA_NOPE_DIM), wkv.dtype)],
                            axis=2).reshape(MLA_KV_RANK, MLA_WIDTH).astype(_BF16)
    w_v_t = wkv[:, :, MLA_NOPE_DIM:].reshape(MLA_KV_RANK, MLA_V_WIDTH).T.astype(_BF16)

    cos_l, sin_l, cos_t, sin_t = _rope_tables(s)
    x2d = x.reshape(b * s, d)
    qt, k, vt, dqt, dk, dvt = _projections(
        x2d, g_attn[0][None, :], w_a, w_dq_t, w_dv_t, g_q_a[0][None, :], w_q_t, g_kv_a[0][None, :],
        w_k_p, w_v_t, cos_l, sin_l, cos_t, sin_t, b, s)

    adiag, acorner, mask = _bias_tables(rel_bias)
    lam_vecs = jnp.stack([lam_q1[0], lam_k1[0], lam_q2[0], lam_k2[0]]).astype(_F32)
    mixed = _attention(qt, k.reshape(b, s, MLA_WIDTH), vt, dqt, dk.reshape(b, s, DIFF_WIDTH), dvt,
                       adiag, acorner, mask, lam_vecs, g_subln[0][None, :])

    out = _out_ffn(mixed.reshape(b * s, MLA_V_WIDTH + DIFF_WIDTH), x2d, w_o[0].astype(_BF16),
                   g_ffn[0][None, :], w_ffn_gate[0].astype(_BF16), w_ffn_up[0].astype(_BF16),
                   w_ffn_down[0].astype(_BF16), g_final[None, :])
    return out.reshape(b, s, d)
```

```python
import math

import numpy as np
import jax
import jax.numpy as jnp
from jax import lax
from jax.experimental import pallas as pl
from jax.experimental.pallas import tpu as pltpu

D_MODEL = 1024
CHUNK = 64
EPS = 1e-6
NEG_INF = -1e30

MLA_HEADS = 8
MLA_NOPE_DIM = 64
MLA_ROPE_DIM = 32
MLA_V_DIM = 64
MLA_Q_RANK = 256
MLA_KV_RANK = 128
ROPE_THETA = 10000.0
ROPE_HALF = MLA_ROPE_DIM // 2

DIFF_HEADS = 4
DIFF_HEAD_DIM = 64
DIFF_V_DIM = 2 * DIFF_HEAD_DIM

REL_BUCKETS = 32
REL_MAX_DIST = 128

FFN_HIDDEN = 2816
LAM_INIT = 0.8 - 0.6 * math.exp(-0.3 * 0)

LOG2E = 1.4426950408889634
MLA_SCALE = (MLA_NOPE_DIM + MLA_ROPE_DIM) ** -0.5
DIFF_SCALE = DIFF_HEAD_DIM ** -0.5

LANES = 128
ONES_ROWS = 16
SCORE_LOOKAHEAD = 2
PROJ_ROWS = 512
ATTN_TILE = 512
FFN_ROWS = 512
FFN_CHUNKS = ((0, 1024), (1024, 2048), (2048, FFN_HIDDEN))
VMEM_LIMIT_BYTES = 56 * 1024 * 1024

MLA_WIDTH = MLA_HEADS * LANES
MLA_V_WIDTH = MLA_HEADS * MLA_V_DIM
DIFF_WIDTH = DIFF_HEADS * LANES

_BF16 = jnp.bfloat16
_F32 = jnp.float32


def _rms(x, g):
    return x * lax.rsqrt(jnp.mean(x * x, axis=-1, keepdims=True) + EPS) * g


def _dot(a, b):
    return jnp.dot(a, b, preferred_element_type=_F32)


def _dot_nt(a, b):
    return lax.dot_general(a, b, (((1,), (1,)), ((), ())), preferred_element_type=_F32)


def _t5_bucket(rel):
    nb = REL_BUCKETS // 2
    max_exact = nb // 2
    ret = (rel > 0).astype(jnp.int32) * nb
    n = jnp.abs(rel)
    nf = jnp.maximum(n, 1).astype(jnp.float32)
    large = max_exact + (jnp.log(nf / max_exact) / math.log(REL_MAX_DIST / max_exact)
                         * (nb - max_exact)).astype(jnp.int32)
    large = jnp.minimum(large, nb - 1)
    return ret + jnp.where(n < max_exact, n, large)


def _bias_kernel(rel_ref, bdiag_ref, bcorner_ref, adiag_ref, acorner_ref, mask_ref):
    t = ATTN_TILE
    key = lax.broadcasted_iota(jnp.int32, (t, t), 0) // CHUNK
    qry = lax.broadcasted_iota(jnp.int32, (t, t), 1) // CHUNK
    mask = jnp.where(key <= qry, 0.0, NEG_INF).astype(_F32)
    mask_ref[...] = mask
    bd = bdiag_ref[...]
    bc = bcorner_ref[...]
    far_bucket = REL_BUCKETS // 2 - 1
    for h in range(DIFF_HEADS):
        far = rel_ref[far_bucket, h]
        vd = jnp.zeros((t, t), _F32)
        vc = jnp.zeros((LANES, LANES), _F32)
        for b in range(REL_BUCKETS):
            val = rel_ref[b, h]
            vd = jnp.where(bd == b, val, vd)
            vc = jnp.where(bc == b, val, vc)
        adiag_ref[h] = (vd - far) * LOG2E + mask
        acorner_ref[h] = (vc - far) * LOG2E


def _bias_tables(rel_bias):
    t = ATTN_TILE
    i = jnp.arange(t)
    bdiag = _t5_bucket(i[:, None] - i[None, :])
    c = jnp.arange(LANES)
    bcorner = _t5_bucket(c[:, None] - c[None, :] - LANES)
    vmem = pl.BlockSpec(memory_space=pltpu.VMEM)
    return pl.pallas_call(
        _bias_kernel,
        out_shape=(jax.ShapeDtypeStruct((DIFF_HEADS, t, t), _F32),
                   jax.ShapeDtypeStruct((DIFF_HEADS, LANES, LANES), _F32),
                   jax.ShapeDtypeStruct((t, t), _F32)),
        in_specs=[pl.BlockSpec(memory_space=pltpu.SMEM), vmem, vmem],
        out_specs=(vmem, vmem, vmem),
        name="bias_tables",
    )(rel_bias, bdiag, bcorner)


def _proj_kernel(x_ref, gattn_ref, wa_ref, wdq_ref, wdv_ref, gq_ref, wq_ref, gkv_ref, wk_ref, wv_ref,
                 cos_ref, sin_ref, cos_t_ref, sin_t_ref,
                 qt_ref, k_ref, vt_ref, dqt_ref, dk_ref, dvt_ref):
    rows = x_ref.shape[0]
    h = _rms(x_ref[...], gattn_ref[...]).astype(_BF16)
    pa = _dot(h, wa_ref[...])
    dk_ref[...] = pa[:, 512:1024].astype(_BF16)
    dqt_ref[0] = (_dot_nt(wdq_ref[...], h) * (DIFF_SCALE * LOG2E)).astype(_BF16)
    dvt_ref[0] = _dot_nt(wdv_ref[...], h).astype(_BF16)

    cq = _rms(pa[:, 0:256], gq_ref[...]).astype(_BF16)
    ckv = _rms(pa[:, 256:384], gkv_ref[...]).astype(_BF16)

    qt = _dot_nt(wq_ref[...], cq)
    cos_t = cos_t_ref[...]
    sin_t = sin_t_ref[...]
    parts = []
    for hd in range(MLA_HEADS):
        b0 = hd * LANES
        r0 = b0 + MLA_NOPE_DIM
        x1 = qt[r0:r0 + ROPE_HALF]
        x2 = qt[r0 + ROPE_HALF:r0 + MLA_ROPE_DIM]
        parts += [qt[b0:r0], x1 * cos_t - x2 * sin_t, x2 * cos_t + x1 * sin_t,
                  qt[r0 + MLA_ROPE_DIM:b0 + LANES]]
    qt_ref[0] = (jnp.concatenate(parts, axis=0) * (MLA_SCALE * LOG2E)).astype(_BF16)

    cos = cos_ref[...]
    sin = sin_ref[...]
    lane = lax.broadcasted_iota(jnp.int32, (rows, LANES), 1)
    first = (lane >= MLA_NOPE_DIM) & (lane < MLA_NOPE_DIM + ROPE_HALF)
    kr = pa[:, 384:512]
    kr = kr * cos + jnp.where(first, pltpu.roll(kr, LANES - ROPE_HALF, 1), pltpu.roll(kr, ROPE_HALF, 1)) * sin
    kk = _dot(ckv, wk_ref[...])
    k_ref[...] = (kk + jnp.concatenate([kr] * MLA_HEADS, axis=1)).astype(_BF16)
    vt_ref[0] = _dot_nt(wv_ref[...], ckv).astype(_BF16)


def _projections(x2d, g_attn, w_a, w_dq_t, w_dv_t, g_q, w_q_t, g_kv, w_k_p, w_v_t,
                 cos_l, sin_l, cos_t, sin_t, batch, seq):
    n = x2d.shape[0]
    r = PROJ_ROWS
    tps = seq // r

    def rows(width):
        return pl.BlockSpec((r, width), lambda i: (i, 0))

    def cols(height):
        return pl.BlockSpec((1, height, r), lambda i: (i // tps, 0, i % tps))

    def whole(a):
        return pl.BlockSpec(a.shape, lambda i: (0, 0))

    pos_rows = pl.BlockSpec((r, LANES), lambda i: (i % tps, 0))
    pos_cols = pl.BlockSpec((ROPE_HALF, r), lambda i: (0, i % tps))
    tok = lambda w: jax.ShapeDtypeStruct((n, w), _BF16)
    feat = lambda hgt: jax.ShapeDtypeStruct((batch, hgt, seq), _BF16)
    return pl.pallas_call(
        _proj_kernel,
        grid=(n // r,),
        in_specs=[rows(D_MODEL), whole(g_attn), whole(w_a), whole(w_dq_t), whole(w_dv_t), whole(g_q),
                  whole(w_q_t), whole(g_kv), whole(w_k_p), whole(w_v_t), pos_rows, pos_rows, pos_cols, pos_cols],
        out_specs=(cols(MLA_WIDTH), rows(MLA_WIDTH), cols(MLA_V_WIDTH), cols(DIFF_WIDTH), rows(DIFF_WIDTH),
                   cols(DIFF_WIDTH)),
        out_shape=(feat(MLA_WIDTH), tok(MLA_WIDTH), feat(MLA_V_WIDTH), feat(DIFF_WIDTH), tok(DIFF_WIDTH),
                   feat(DIFF_WIDTH)),
        compiler_params=pltpu.CompilerParams(dimension_semantics=("parallel",),
                                             vmem_limit_bytes=VMEM_LIMIT_BYTES),
        name="projections",
    )(x2d, g_attn, w_a, w_dq_t, w_dv_t, g_q, w_q_t, g_kv, w_k_p, w_v_t, cos_l, sin_l, cos_t, sin_t)


def _softmax_pv_step(s_t, v_t, m_ref, acc_ref, h):
    m_prev = m_ref[h]
    m_new = jnp.maximum(m_prev, jnp.max(s_t, axis=0, keepdims=True))
    alpha = jnp.exp2(m_prev - m_new)
    p_t = jnp.exp2(s_t - m_new).astype(_BF16)
    ones = (lax.broadcasted_iota(jnp.int32, (ONES_ROWS, v_t.shape[1]), 0) == 0).astype(_BF16)
    v_aug = jnp.concatenate([v_t, ones], axis=0)
    acc_ref[h] = alpha * acc_ref[h] + _dot(v_aug, p_t)
    m_ref[h] = m_new


def _attn_kernel(qi_tab, ki_tab,
                 qt_ref, k_ref, vt_ref, dqt_ref, dk_ref, dvt_ref,
                 adiag_ref, acorner_ref, mask_ref, lam_ref, gsub_ref,
                 out_ref,
                 m_a, acc_a, m_d, acc_d):
    t = ATTN_TILE
    step = pl.program_id(1)
    qi = qi_tab[step]
    ki = ki_tab[step]

    @pl.when(ki == 0)
    def _init():
        m_a[...] = jnp.full(m_a.shape, -jnp.inf, _F32)
        acc_a[...] = jnp.zeros(acc_a.shape, _F32)
        m_d[...] = jnp.full(m_d.shape, -jnp.inf, _F32)
        acc_d[...] = jnp.zeros(acc_d.shape, _F32)

    def add_corner(s_t, corner):
        bot = s_t[t - LANES:]
        bot = jnp.concatenate([bot[:, :LANES] + corner, bot[:, LANES:t],
                               bot[:, t:t + LANES] + corner, bot[:, t + LANES:]], axis=1)
        return jnp.concatenate([s_t[:t - LANES], bot], axis=0)

    def scores(unit):
        if unit < MLA_HEADS:
            blk = slice(unit * LANES, (unit + 1) * LANES)
            return _dot(k_ref[0, :, blk], qt_ref[0, blk, :])
        h = unit - MLA_HEADS
        blk = slice(h * LANES, (h + 1) * LANES)
        feat = lax.broadcasted_iota(jnp.int32, (LANES, t), 0)
        q = dqt_ref[0, blk, :]
        zero = jnp.zeros_like(q)
        q12 = jnp.concatenate([jnp.where(feat < DIFF_HEAD_DIM, q, zero),
                               jnp.where(feat >= DIFF_HEAD_DIM, q, zero)], axis=1)
        return _dot(dk_ref[0, :, blk], q12)

    def softmax_pv(unit, s_t, kind):
        if unit < MLA_HEADS:
            if kind == "diag":
                s_t = s_t + mask_ref[...]
            vrows = slice(unit * MLA_V_DIM, (unit + 1) * MLA_V_DIM)
            _softmax_pv_step(s_t, vt_ref[0, vrows, :], m_a, acc_a, unit)
        else:
            h = unit - MLA_HEADS
            if kind == "diag":
                a = adiag_ref[h]
                s_t = s_t + jnp.concatenate([a, a], axis=1)
            elif kind == "sub":
                s_t = add_corner(s_t, acorner_ref[h])
            _softmax_pv_step(s_t, dvt_ref[0, h * LANES:(h + 1) * LANES, :], m_d, acc_d, h)

    def tile_step(kind):
        units = MLA_HEADS + DIFF_HEADS
        pending = [scores(u) for u in range(SCORE_LOOKAHEAD)]
        for unit in range(units):
            if unit + SCORE_LOOKAHEAD < units:
                pending.append(scores(unit + SCORE_LOOKAHEAD))
            softmax_pv(unit, pending.pop(0), kind)

    @pl.when(ki < qi - 1)
    def _far():
        tile_step("far")

    @pl.when(ki == qi - 1)
    def _sub():
        tile_step("sub")

    @pl.when(ki == qi)
    def _diag():
        tile_step("diag")
        for j in range(MLA_HEADS // 2):
            a0 = acc_a[2 * j]
            a1 = acc_a[2 * j + 1]
            o_t = jnp.concatenate([a0[:MLA_V_DIM] * (1.0 / a0[MLA_V_DIM:MLA_V_DIM + 1]),
                                   a1[:MLA_V_DIM] * (1.0 / a1[MLA_V_DIM:MLA_V_DIM + 1])], axis=0)
            out_ref[0, :, j * LANES:(j + 1) * LANES] = o_t.T.astype(_BF16)
        lam = (jnp.exp(jnp.sum(lam_ref[0:1, :] * lam_ref[1:2, :], axis=-1, keepdims=True))
               - jnp.exp(jnp.sum(lam_ref[2:3, :] * lam_ref[3:4, :], axis=-1, keepdims=True))
               + LAM_INIT)
        base = MLA_V_WIDTH
        for h in range(DIFF_HEADS):
            acc = acc_d[h]
            inv = 1.0 / acc[DIFF_V_DIM:DIFF_V_DIM + 1]
            o_t = acc[:DIFF_V_DIM, :t] * inv[:, :t] - lam * (acc[:DIFF_V_DIM, t:] * inv[:, t:])
            y = _rms(o_t.T, gsub_ref[...]) * (1.0 - LAM_INIT)
            out_ref[0, :, base + h * LANES:base + (h + 1) * LANES] = y.astype(_BF16)


def _attention(qt, k, vt, dqt, dk, dvt, adiag, acorner, mask, lam_vecs, g_sub):
    b, s, _ = k.shape
    t = ATTN_TILE
    nq = s // t
    pairs = [(qi, ki) for qi in range(nq) for ki in range(qi + 1)]
    qi_tab = jnp.asarray(np.array([p[0] for p in pairs], np.int32))
    ki_tab = jnp.asarray(np.array([p[1] for p in pairs], np.int32))

    def tok(width, tab):
        return pl.BlockSpec((1, t, width), lambda bi, p, qt_, kt_: (bi, (qt_, kt_)[tab][p], 0))

    def feat(height, tab):
        return pl.BlockSpec((1, height, t), lambda bi, p, qt_, kt_: (bi, 0, (qt_, kt_)[tab][p]))

    def const(a):
        nd = a.ndim
        return pl.BlockSpec(a.shape, lambda bi, p, qt_, kt_: (0,) * nd)

    grid_spec = pltpu.PrefetchScalarGridSpec(
        num_scalar_prefetch=2,
        grid=(b, len(pairs)),
        in_specs=[feat(MLA_WIDTH, 0), tok(MLA_WIDTH, 1), feat(MLA_V_WIDTH, 1),
                  feat(DIFF_WIDTH, 0), tok(DIFF_WIDTH, 1), feat(DIFF_WIDTH, 1),
                  const(adiag), const(acorner), const(mask), const(lam_vecs), const(g_sub)],
        out_specs=tok(MLA_V_WIDTH + DIFF_WIDTH, 0),
        scratch_shapes=[
            pltpu.VMEM((MLA_HEADS, 1, t), _F32),
            pltpu.VMEM((MLA_HEADS, MLA_V_DIM + ONES_ROWS, t), _F32),
            pltpu.VMEM((DIFF_HEADS, 1, 2 * t), _F32),
            pltpu.VMEM((DIFF_HEADS, DIFF_V_DIM + ONES_ROWS, 2 * t), _F32),
        ],
    )
    return pl.pallas_call(
        _attn_kernel,
        grid_spec=grid_spec,
        out_shape=jax.ShapeDtypeStruct((b, s, MLA_V_WIDTH + DIFF_WIDTH), _BF16),
        compiler_params=pltpu.CompilerParams(dimension_semantics=("parallel", "arbitrary"),
                                             vmem_limit_bytes=VMEM_LIMIT_BYTES),
        name="attention",
    )(qi_tab, ki_tab, qt, k, vt, dqt, dk, dvt, adiag, acorner, mask, lam_vecs, g_sub)


def _ffn_kernel(mixed_ref, x_ref, wo_ref, gffn_ref, wg_ref, wu_ref, wd_ref, gfin_ref, out_ref):
    x1 = x_ref[...] + _dot(mixed_ref[...], wo_ref[...])
    h = _rms(x1, gffn_ref[...]).astype(_BF16)
    y = jnp.zeros_like(x1)
    for c0, c1 in FFN_CHUNKS:
        g = _dot(h, wg_ref[:, c0:c1])
        u = _dot(h, wu_ref[:, c0:c1])
        act = (g * (1.0 / (1.0 + jnp.exp(-g))) * u).astype(_BF16)
        y = y + _dot(act, wd_ref[c0:c1, :])
    out_ref[...] = _rms(x1 + y, gfin_ref[...])


def _out_ffn(mixed2d, x2d, w_o, g_ffn, w_g, w_u, w_d, g_fin):
    n = x2d.shape[0]
    r = FFN_ROWS

    def rows(width):
        return pl.BlockSpec((r, width), lambda i: (i, 0))

    def resident(a):
        return pl.BlockSpec(a.shape, lambda i: (0, 0), pipeline_mode=pl.Buffered(1))

    return pl.pallas_call(
        _ffn_kernel,
        grid=(n // r,),
        in_specs=[rows(mixed2d.shape[1]), rows(D_MODEL), resident(w_o), resident(g_ffn), resident(w_g),
                  resident(w_u), resident(w_d), resident(g_fin)],
        out_specs=rows(D_MODEL),
        out_shape=jax.ShapeDtypeStruct((n, D_MODEL), _F32),
        compiler_params=pltpu.CompilerParams(dimension_semantics=("parallel",),
                                             vmem_limit_bytes=VMEM_LIMIT_BYTES),
        name="out_ffn",
    )(mixed2d, x2d, w_o, g_ffn, w_g, w_u, w_d, g_fin)


def _rope_tables(seq):
    inv_freq = ROPE_THETA ** (-jnp.arange(ROPE_HALF, dtype=jnp.float32) / ROPE_HALF)
    ang = jnp.arange(seq).astype(jnp.float32)[:, None] * inv_freq[None, :]
    cos = jnp.cos(ang)
    sin = jnp.sin(ang)
    one = jnp.ones((seq, MLA_NOPE_DIM), _F32)
    zero = jnp.zeros((seq, MLA_NOPE_DIM), _F32)
    pad = LANES - MLA_NOPE_DIM - MLA_ROPE_DIM
    cos_l = jnp.concatenate([one, cos, cos, one[:, :pad]], axis=1)
    sin_l = jnp.concatenate([zero, -sin, sin, zero[:, :pad]], axis=1)
    return cos_l, sin_l, cos.T, sin.T


def kernel(x, w_in, g_attn, g_q_a, w_q_b, g_kv_a, w_kv_b, lam_q1, lam_k1, lam_q2, lam_k2, g_subln,
           rel_bias, w_o, g_ffn, w_ffn_gate, w_ffn_up, w_ffn_down, g_final):
    b, s, d = x.shape
    assert d == D_MODEL and s % ATTN_TILE == 0 and s % PROJ_ROWS == 0 and (b * s) % FFN_ROWS == 0
    assert w_in.shape[0] == 1, "single layer"

    w = w_in[0]
    zc = lambda n: jnp.zeros((D_MODEL, n), w.dtype)
    pad = LANES - MLA_NOPE_DIM - MLA_ROPE_DIM
    w_a = jnp.concatenate([w[:, 0:384], zc(MLA_NOPE_DIM), w[:, 384:416], zc(pad), w[:, 928:1440]],
                          axis=1).astype(_BF16)
    w_dq_t = w[:, 416:928].T.astype(_BF16)
    w_dv_t = w[:, 1440:1952].T.astype(_BF16)
    wq = w_q_b[0].reshape(MLA_Q_RANK, MLA_HEADS, MLA_NOPE_DIM + MLA_ROPE_DIM)
    w_q_t = jnp.concatenate([wq, jnp.zeros((MLA_Q_RANK, MLA_HEADS, pad), wq.dtype)],
                            axis=2).reshape(MLA_Q_RANK, MLA_WIDTH).T.astype(_BF16)
    wkv = w_kv_b[0].reshape(MLA_KV_RANK, MLA_HEADS, MLA_NOPE_DIM + MLA_V_DIM)
    w_k_p = jnp.concatenate([wkv[:, :, :MLA_NOPE_DIM],
                             jnp.zeros((MLA_KV_RANK, MLA_HEADS, LANES - MLA_NOPE_DIM), wkv.dtype)],
                            axis=2).reshape(MLA_KV_RANK, MLA_WIDTH).astype(_BF16)
    w_v_t = wkv[:, :, MLA_NOPE_DIM:].reshape(MLA_KV_RANK, MLA_V_WIDTH).T.astype(_BF16)

    cos_l, sin_l, cos_t, sin_t = _rope_tables(s)
    x2d = x.reshape(b * s, d)
    qt, k, vt, dqt, dk, dvt = _projections(
        x2d, g_attn[0][None, :], w_a, w_dq_t, w_dv_t, g_q_a[0][None, :], w_q_t, g_kv_a[0][None, :],
        w_k_p, w_v_t, cos_l, sin_l, cos_t, sin_t, b, s)

    adiag, acorner, mask = _bias_tables(rel_bias)
    lam_vecs = jnp.stack([lam_q1[0], lam_k1[0], lam_q2[0], lam_k2[0]]).astype(_F32)
    mixed = _attention(qt, k.reshape(b, s, MLA_WIDTH), vt, dqt, dk.reshape(b, s, DIFF_WIDTH), dvt,
                       adiag, acorner, mask, lam_vecs, g_subln[0][None, :])

    out = _out_ffn(mixed.reshape(b * s, MLA_V_WIDTH + DIFF_WIDTH), x2d, w_o[0].astype(_BF16),
                   g_ffn[0][None, :], w_ffn_gate[0].astype(_BF16), w_ffn_up[0].astype(_BF16),
                   w_ffn_down[0].astype(_BF16), g_final[None, :])
    return out.reshape(b, s, d)
```

```python
import math

import jax
import jax.numpy as jnp
from jax import lax
from jax.experimental import pallas as pl
from jax.experimental.pallas import tpu as pltpu

D_MODEL = 1024
CHUNK = 64
EPS = 1e-6
NEG_INF = -1e30

MLA_HEADS = 8
MLA_NOPE_DIM = 64
MLA_ROPE_DIM = 32
MLA_V_DIM = 64
MLA_Q_RANK = 256
MLA_KV_RANK = 128
ROPE_THETA = 10000.0
ROPE_HALF = MLA_ROPE_DIM // 2

DIFF_HEADS = 4
DIFF_HEAD_DIM = 64
DIFF_V_DIM = 2 * DIFF_HEAD_DIM

REL_BUCKETS = 32
REL_MAX_DIST = 128

FFN_HIDDEN = 2816
LAM_INIT = 0.8 - 0.6 * math.exp(-0.3 * 0)

LOG2E = 1.4426950408889634
MLA_SCALE = (MLA_NOPE_DIM + MLA_ROPE_DIM) ** -0.5
DIFF_SCALE = DIFF_HEAD_DIM ** -0.5

LANES = 128
ONES_ROWS = 16
SCORE_LOOKAHEAD = 2
PROJ_ROWS = 512
ATTN_TILE = 512
FFN_ROWS = 512
FFN_CHUNKS = ((0, 1024), (1024, 2048), (2048, FFN_HIDDEN))
VMEM_LIMIT_BYTES = 56 * 1024 * 1024

MLA_WIDTH = MLA_HEADS * LANES
MLA_V_WIDTH = MLA_HEADS * MLA_V_DIM
DIFF_WIDTH = DIFF_HEADS * LANES

_BF16 = jnp.bfloat16
_F32 = jnp.float32


def _rms(x, g):
    return x * lax.rsqrt(jnp.mean(x * x, axis=-1, keepdims=True) + EPS) * g


def _dot(a, b):
    return jnp.dot(a, b, preferred_element_type=_F32)


def _dot_nt(a, b):
    return lax.dot_general(a, b, (((1,), (1,)), ((), ())), preferred_element_type=_F32)


def _t5_bucket(rel):
    nb = REL_BUCKETS // 2
    max_exact = nb // 2
    ret = (rel > 0).astype(jnp.int32) * nb
    n = jnp.abs(rel)
    nf = jnp.maximum(n, 1).astype(jnp.float32)
    large = max_exact + (jnp.log(nf / max_exact) / math.log(REL_MAX_DIST / max_exact)
                         * (nb - max_exact)).astype(jnp.int32)
    large = jnp.minimum(large, nb - 1)
    return ret + jnp.where(n < max_exact, n, large)


def _bias_kernel(rel_ref, bdiag_ref, bcorner_ref, adiag_ref, acorner_ref, mask_ref):
    t = ATTN_TILE
    key = lax.broadcasted_iota(jnp.int32, (t, t), 0) // CHUNK
    qry = lax.broadcasted_iota(jnp.int32, (t, t), 1) // CHUNK
    mask = jnp.where(key <= qry, 0.0, NEG_INF).astype(_F32)
    mask_ref[...] = mask
    bd = bdiag_ref[...]
    bc = bcorner_ref[...]
    far_bucket = REL_BUCKETS // 2 - 1
    for h in range(DIFF_HEADS):
        far = rel_ref[far_bucket, h]
        vd = jnp.zeros((t, t), _F32)
        vc = jnp.zeros((LANES, LANES), _F32)
        for b in range(REL_BUCKETS):
            val = rel_ref[b, h]
            vd = jnp.where(bd == b, val, vd)
            vc = jnp.where(bc == b, val, vc)
        adiag_ref[h] = (vd - far) * LOG2E + mask
        acorner_ref[h] = (vc - far) * LOG2E


def _bias_tables(rel_bias):
    t = ATTN_TILE
    i = jnp.arange(t)
    bdiag = _t5_bucket(i[:, None] - i[None, :])
    c = jnp.arange(LANES)
    bcorner = _t5_bucket(c[:, None] - c[None, :] - LANES)
    vmem = pl.BlockSpec(memory_space=pltpu.VMEM)
    return pl.pallas_call(
        _bias_kernel,
        out_shape=(jax.ShapeDtypeStruct((DIFF_HEADS, t, t), _F32),
                   jax.ShapeDtypeStruct((DIFF_HEADS, LANES, LANES), _F32),
                   jax.ShapeDtypeStruct((t, t), _F32)),
        in_specs=[pl.BlockSpec(memory_space=pltpu.SMEM), vmem, vmem],
        out_specs=(vmem, vmem, vmem),
        name="bias_tables",
    )(rel_bias, bdiag, bcorner)


def _proj_kernel(x_ref, gattn_ref, wa_ref, wdq_ref, wdv_ref, gq_ref, wq_ref, gkv_ref, wk_ref, wv_ref,
                 cos_ref, sin_ref, cos_t_ref, sin_t_ref,
                 qt_ref, k_ref, vt_ref, dqt_ref, dk_ref, dvt_ref):
    rows = x_ref.shape[0]
    h = _rms(x_ref[...], gattn_ref[...]).astype(_BF16)
    pa = _dot(h, wa_ref[...])
    dk_ref[...] = pa[:, 512:1024].astype(_BF16)
    dqt_ref[0, 0] = (_dot_nt(wdq_ref[...], h) * (DIFF_SCALE * LOG2E)).astype(_BF16)
    dvt_ref[0, 0] = _dot_nt(wdv_ref[...], h).astype(_BF16)

    cq = _rms(pa[:, 0:256], gq_ref[...]).astype(_BF16)
    ckv = _rms(pa[:, 256:384], gkv_ref[...]).astype(_BF16)

    qt = _dot_nt(wq_ref[...], cq)
    cos_t = cos_t_ref[...]
    sin_t = sin_t_ref[...]
    parts = []
    for hd in range(MLA_HEADS):
        b0 = hd * LANES
        r0 = b0 + MLA_NOPE_DIM
        x1 = qt[r0:r0 + ROPE_HALF]
        x2 = qt[r0 + ROPE_HALF:r0 + MLA_ROPE_DIM]
        parts += [qt[b0:r0], x1 * cos_t - x2 * sin_t, x2 * cos_t + x1 * sin_t,
                  qt[r0 + MLA_ROPE_DIM:b0 + LANES]]
    qt_ref[0, 0] = (jnp.concatenate(parts, axis=0) * (MLA_SCALE * LOG2E)).astype(_BF16)

    cos = cos_ref[...]
    sin = sin_ref[...]
    lane = lax.broadcasted_iota(jnp.int32, (rows, LANES), 1)
    first = (lane >= MLA_NOPE_DIM) & (lane < MLA_NOPE_DIM + ROPE_HALF)
    kr = pa[:, 384:512]
    kr = kr * cos + jnp.where(first, pltpu.roll(kr, LANES - ROPE_HALF, 1), pltpu.roll(kr, ROPE_HALF, 1)) * sin
    kk = _dot(ckv, wk_ref[...])
    k_ref[...] = (kk + jnp.concatenate([kr] * MLA_HEADS, axis=1)).astype(_BF16)
    vt_ref[0, 0] = _dot_nt(wv_ref[...], ckv).astype(_BF16)


def _projections(x2d, g_attn, w_a, w_dq_t, w_dv_t, g_q, w_q_t, g_kv, w_k_p, w_v_t,
                 cos_l, sin_l, cos_t, sin_t, batch, seq):
    n = x2d.shape[0]
    r = PROJ_ROWS
    tps = seq // r

    def rows(width):
        return pl.BlockSpec((r, width), lambda i: (i, 0))

    def cols(height):
        return pl.BlockSpec((1, 1, height, r), lambda i: (i // tps, i % tps, 0, 0))

    def whole(a):
        return pl.BlockSpec(a.shape, lambda i: (0, 0))

    pos_rows = pl.BlockSpec((r, LANES), lambda i: (i % tps, 0))
    pos_cols = pl.BlockSpec((ROPE_HALF, r), lambda i: (0, i % tps))
    tok = lambda w: jax.ShapeDtypeStruct((n, w), _BF16)
    feat = lambda hgt: jax.ShapeDtypeStruct((batch, tps, hgt, r), _BF16)
    return pl.pallas_call(
        _proj_kernel,
        grid=(n // r,),
        in_specs=[rows(D_MODEL), whole(g_attn), whole(w_a), whole(w_dq_t), whole(w_dv_t), whole(g_q),
                  whole(w_q_t), whole(g_kv), whole(w_k_p), whole(w_v_t), pos_rows, pos_rows, pos_cols, pos_cols],
        out_specs=(cols(MLA_WIDTH), rows(MLA_WIDTH), cols(MLA_V_WIDTH), cols(DIFF_WIDTH), rows(DIFF_WIDTH),
                   cols(DIFF_WIDTH)),
        out_shape=(feat(MLA_WIDTH), tok(MLA_WIDTH), feat(MLA_V_WIDTH), feat(DIFF_WIDTH), tok(DIFF_WIDTH),
                   feat(DIFF_WIDTH)),
        compiler_params=pltpu.CompilerParams(dimension_semantics=("parallel",),
                                             vmem_limit_bytes=VMEM_LIMIT_BYTES),
        name="projections",
    )(x2d, g_attn, w_a, w_dq_t, w_dv_t, g_q, w_q_t, g_kv, w_k_p, w_v_t, cos_l, sin_l, cos_t, sin_t)


def _softmax_pv_step(s_t, v_t, m_ref, acc_ref, h):
    m_prev = m_ref[h]
    m_new = jnp.maximum(m_prev, jnp.max(s_t, axis=0, keepdims=True))
    alpha = jnp.exp2(m_prev - m_new)
    p_t = jnp.exp2(s_t - m_new).astype(_BF16)
    ones = (lax.broadcasted_iota(jnp.int32, (ONES_ROWS, v_t.shape[1]), 0) == 0).astype(_BF16)
    v_aug = jnp.concatenate([v_t, ones], axis=0)
    acc_ref[h] = alpha * acc_ref[h] + _dot(v_aug, p_t)
    m_ref[h] = m_new


def _attn_kernel(qt_ref, k_ref, vt_ref, dqt_ref, dk_ref, dvt_ref,
                 adiag_ref, acorner_ref, mask_ref, lam_ref, gsub_ref,
                 out_ref,
                 m_a, acc_a, m_d, acc_d):
    t = ATTN_TILE
    qi = pl.program_id(1)

    m_a[...] = jnp.full(m_a.shape, -jnp.inf, _F32)
    acc_a[...] = jnp.zeros(acc_a.shape, _F32)
    m_d[...] = jnp.full(m_d.shape, -jnp.inf, _F32)
    acc_d[...] = jnp.zeros(acc_d.shape, _F32)

    def add_corner(s_t, corner):
        bot = s_t[t - LANES:]
        bot = jnp.concatenate([bot[:, :LANES] + corner, bot[:, LANES:t],
                               bot[:, t:t + LANES] + corner, bot[:, t + LANES:]], axis=1)
        return jnp.concatenate([s_t[:t - LANES], bot], axis=0)

    def scores(unit, ki):
        if unit < MLA_HEADS:
            blk = slice(unit * LANES, (unit + 1) * LANES)
            return _dot(k_ref[0, ki, :, blk], qt_ref[0, 0, blk, :])
        h = unit - MLA_HEADS
        blk = slice(h * LANES, (h + 1) * LANES)
        feat = lax.broadcasted_iota(jnp.int32, (LANES, t), 0)
        q = dqt_ref[0, 0, blk, :]
        zero = jnp.zeros_like(q)
        q12 = jnp.concatenate([jnp.where(feat < DIFF_HEAD_DIM, q, zero),
                               jnp.where(feat >= DIFF_HEAD_DIM, q, zero)], axis=1)
        return _dot(dk_ref[0, ki, :, blk], q12)

    def softmax_pv(unit, s_t, kind, ki):
        if unit < MLA_HEADS:
            if kind == "diag":
                s_t = s_t + mask_ref[...]
            vrows = slice(unit * MLA_V_DIM, (unit + 1) * MLA_V_DIM)
            _softmax_pv_step(s_t, vt_ref[0, ki, vrows, :], m_a, acc_a, unit)
        else:
            h = unit - MLA_HEADS
            if kind == "diag":
                a = adiag_ref[h]
                s_t = s_t + jnp.concatenate([a, a], axis=1)
            elif kind == "sub":
                s_t = add_corner(s_t, acorner_ref[h])
            _softmax_pv_step(s_t, dvt_ref[0, ki, h * LANES:(h + 1) * LANES, :], m_d, acc_d, h)

    def tile_step(kind, ki):
        units = MLA_HEADS + DIFF_HEADS
        pending = [scores(u, ki) for u in range(SCORE_LOOKAHEAD)]
        for unit in range(units):
            if unit + SCORE_LOOKAHEAD < units:
                pending.append(scores(unit + SCORE_LOOKAHEAD, ki))
            softmax_pv(unit, pending.pop(0), kind, ki)

    def far_body(ki, carry):
        tile_step("far", ki)
        return carry

    lax.fori_loop(0, jnp.maximum(qi - 1, 0), far_body, 0)

    @pl.when(qi >= 1)
    def _sub():
        tile_step("sub", qi - 1)

    tile_step("diag", qi)

    for j in range(MLA_HEADS // 2):
        a0 = acc_a[2 * j]
        a1 = acc_a[2 * j + 1]
        o_t = jnp.concatenate([a0[:MLA_V_DIM] * (1.0 / a0[MLA_V_DIM:MLA_V_DIM + 1]),
                               a1[:MLA_V_DIM] * (1.0 / a1[MLA_V_DIM:MLA_V_DIM + 1])], axis=0)
        out_ref[0, :, j * LANES:(j + 1) * LANES] = o_t.T.astype(_BF16)
    lam = (jnp.exp(jnp.sum(lam_ref[0:1, :] * lam_ref[1:2, :], axis=-1, keepdims=True))
           - jnp.exp(jnp.sum(lam_ref[2:3, :] * lam_ref[3:4, :], axis=-1, keepdims=True))
           + LAM_INIT)
    base = MLA_V_WIDTH
    for h in range(DIFF_HEADS):
        acc = acc_d[h]
        inv = 1.0 / acc[DIFF_V_DIM:DIFF_V_DIM + 1]
        o_t = acc[:DIFF_V_DIM, :t] * inv[:, :t] - lam * (acc[:DIFF_V_DIM, t:] * inv[:, t:])
        y = _rms(o_t.T, gsub_ref[...]) * (1.0 - LAM_INIT)
        out_ref[0, :, base + h * LANES:base + (h + 1) * LANES] = y.astype(_BF16)


def _attention(qt, k, vt, dqt, dk, dvt, adiag, acorner, mask, lam_vecs, g_sub):
    b, nk, t, _ = k.shape
    assert t == ATTN_TILE
    s = nk * t

    def q_tile(a):
        return pl.BlockSpec((1, 1) + a.shape[2:], lambda bi, qi: (bi, qi, 0, 0))

    def resident(a):
        return pl.BlockSpec((1,) + a.shape[1:], lambda bi, qi: (bi, 0, 0, 0), pipeline_mode=pl.Buffered(1))

    def const(a):
        nd = a.ndim
        return pl.BlockSpec(a.shape, lambda bi, qi: (0,) * nd, pipeline_mode=pl.Buffered(1))

    width = MLA_V_WIDTH + DIFF_WIDTH
    return pl.pallas_call(
        _attn_kernel,
        grid=(b, nk),
        in_specs=[q_tile(qt), resident(k), resident(vt), q_tile(dqt), resident(dk), resident(dvt),
                  const(adiag), const(acorner), const(mask), const(lam_vecs), const(g_sub)],
        out_specs=pl.BlockSpec((1, t, width), lambda bi, qi: (bi, qi, 0)),
        scratch_shapes=[
            pltpu.VMEM((MLA_HEADS, 1, t), _F32),
            pltpu.VMEM((MLA_HEADS, MLA_V_DIM + ONES_ROWS, t), _F32),
            pltpu.VMEM((DIFF_HEADS, 1, 2 * t), _F32),
            pltpu.VMEM((DIFF_HEADS, DIFF_V_DIM + ONES_ROWS, 2 * t), _F32),
        ],
        out_shape=jax.ShapeDtypeStruct((b, s, width), _BF16),
        compiler_params=pltpu.CompilerParams(dimension_semantics=("parallel", "arbitrary"),
                                             vmem_limit_bytes=VMEM_LIMIT_BYTES),
        name="attention",
    )(qt, k, vt, dqt, dk, dvt, adiag, acorner, mask, lam_vecs, g_sub)


def _ffn_kernel(mixed_ref, x_ref, wo_ref, gffn_ref, wg_ref, wu_ref, wd_ref, gfin_ref, out_ref):
    x1 = x_ref[...] + _dot(mixed_ref[...], wo_ref[...])
    h = _rms(x1, gffn_ref[...]).astype(_BF16)
    y = jnp.zeros_like(x1)
    for c0, c1 in FFN_CHUNKS:
        g = _dot(h, wg_ref[:, c0:c1])
        u = _dot(h, wu_ref[:, c0:c1])
        act = (g * (1.0 / (1.0 + jnp.exp(-g))) * u).astype(_BF16)
        y = y + _dot(act, wd_ref[c0:c1, :])
    out_ref[...] = _rms(x1 + y, gfin_ref[...])


def _out_ffn(mixed2d, x2d, w_o, g_ffn, w_g, w_u, w_d, g_fin):
    n = x2d.shape[0]
    r = FFN_ROWS

    def rows(width):
        return pl.BlockSpec((r, width), lambda i: (i, 0))

    def resident(a):
        return pl.BlockSpec(a.shape, lambda i: (0, 0), pipeline_mode=pl.Buffered(1))

    return pl.pallas_call(
        _ffn_kernel,
        grid=(n // r,),
        in_specs=[rows(mixed2d.shape[1]), rows(D_MODEL), resident(w_o), resident(g_ffn), resident(w_g),
                  resident(w_u), resident(w_d), resident(g_fin)],
        out_specs=rows(D_MODEL),
        out_shape=jax.ShapeDtypeStruct((n, D_MODEL), _F32),
        compiler_params=pltpu.CompilerParams(dimension_semantics=("parallel",),
                                             vmem_limit_bytes=VMEM_LIMIT_BYTES),
        name="out_ffn",
    )(mixed2d, x2d, w_o, g_ffn, w_g, w_u, w_d, g_fin)


def _rope_tables(seq):
    inv_freq = ROPE_THETA ** (-jnp.arange(ROPE_HALF, dtype=jnp.float32) / ROPE_HALF)
    ang = jnp.arange(seq).astype(jnp.float32)[:, None] * inv_freq[None, :]
    cos = jnp.cos(ang)
    sin = jnp.sin(ang)
    one = jnp.ones((seq, MLA_NOPE_DIM), _F32)
    zero = jnp.zeros((seq, MLA_NOPE_DIM), _F32)
    pad = LANES - MLA_NOPE_DIM - MLA_ROPE_DIM
    cos_l = jnp.concatenate([one, cos, cos, one[:, :pad]], axis=1)
    sin_l = jnp.concatenate([zero, -sin, sin, zero[:, :pad]], axis=1)
    return cos_l, sin_l, cos.T, sin.T


def kernel(x, w_in, g_attn, g_q_a, w_q_b, g_kv_a, w_kv_b, lam_q1, lam_k1, lam_q2, lam_k2, g_subln,
           rel_bias, w_o, g_ffn, w_ffn_gate, w_ffn_up, w_ffn_down, g_final):
    b, s, d = x.shape
    assert d == D_MODEL and s % ATTN_TILE == 0 and PROJ_ROWS == ATTN_TILE and (b * s) % FFN_ROWS == 0
    assert w_in.shape[0] == 1, "single layer"

    w = w_in[0]
    zc = lambda n: jnp.zeros((D_MODEL, n), w.dtype)
    pad = LANES - MLA_NOPE_DIM - MLA_ROPE_DIM
    w_a = jnp.concatenate([w[:, 0:384], zc(MLA_NOPE_DIM), w[:, 384:416], zc(pad), w[:, 928:1440]],
                          axis=1).astype(_BF16)
    w_dq_t = w[:, 416:928].T.astype(_BF16)
    w_dv_t = w[:, 1440:1952].T.astype(_BF16)
    wq = w_q_b[0].reshape(MLA_Q_RANK, MLA_HEADS, MLA_NOPE_DIM + MLA_ROPE_DIM)
    w_q_t = jnp.concatenate([wq, jnp.zeros((MLA_Q_RANK, MLA_HEADS, pad), wq.dtype)],
                            axis=2).reshape(MLA_Q_RANK, MLA_WIDTH).T.astype(_BF16)
    wkv = w_kv_b[0].reshape(MLA_KV_RANK, MLA_HEADS, MLA_NOPE_DIM + MLA_V_DIM)
    w_k_p = jnp.concatenate([wkv[:, :, :MLA_NOPE_DIM],
                             jnp.zeros((MLA_KV_RANK, MLA_HEADS, LANES - MLA_NOPE_DIM), wkv.dtype)],
                            axis=2).reshape(MLA_KV_RANK, MLA_WIDTH).astype(_BF16)
    w_v_t = wkv[:, :, MLA_NOPE_DIM:].reshape(MLA_KV_RANK, MLA_V_WIDTH).T.astype(_BF16)

    cos_l, sin_l, cos_t, sin_t = _rope_tables(s)
    x2d = x.reshape(b * s, d)
    qt, k, vt, dqt, dk, dvt = _projections(
        x2d, g_attn[0][None, :], w_a, w_dq_t, w_dv_t, g_q_a[0][None, :], w_q_t, g_kv_a[0][None, :],
        w_k_p, w_v_t, cos_l, sin_l, cos_t, sin_t, b, s)

    adiag, acorner, mask = _bias_tables(rel_bias)
    lam_vecs = jnp.stack([lam_q1[0], lam_k1[0], lam_q2[0], lam_k2[0]]).astype(_F32)
    nk = s // ATTN_TILE
    mixed = _attention(qt, k.reshape(b, nk, ATTN_TILE, MLA_WIDTH), vt, dqt,
                       dk.reshape(b, nk, ATTN_TILE, DIFF_WIDTH), dvt,
                       adiag, acorner, mask, lam_vecs, g_subln[0][None, :])

    out = _out_ffn(mixed.reshape(b * s, MLA_V_WIDTH + DIFF_WIDTH), x2d, w_o[0].astype(_BF16),
                   g_ffn[0][None, :], w_ffn_gate[0].astype(_BF16), w_ffn_up[0].astype(_BF16),
                   w_ffn_down[0].astype(_BF16), g_final[None, :])
    return out.reshape(b, s, d)
```

```python
import math

import jax
import jax.numpy as jnp
from jax import lax
from jax.experimental import pallas as pl
from jax.experimental.pallas import tpu as pltpu

D_MODEL = 1024
CHUNK = 64
EPS = 1e-6
NEG_INF = -1e30

MLA_HEADS = 8
MLA_NOPE_DIM = 64
MLA_ROPE_DIM = 32
MLA_V_DIM = 64
MLA_Q_RANK = 256
MLA_KV_RANK = 128
ROPE_THETA = 10000.0
ROPE_HALF = MLA_ROPE_DIM // 2

DIFF_HEADS = 4
DIFF_HEAD_DIM = 64
DIFF_V_DIM = 2 * DIFF_HEAD_DIM

REL_BUCKETS = 32
REL_MAX_DIST = 128

FFN_HIDDEN = 2816
LAM_INIT = 0.8 - 0.6 * math.exp(-0.3 * 0)

LOG2E = 1.4426950408889634
MLA_SCALE = (MLA_NOPE_DIM + MLA_ROPE_DIM) ** -0.5
DIFF_SCALE = DIFF_HEAD_DIM ** -0.5

LANES = 128
ONES_ROWS = 16
QCOLS = 256
SCORE_LOOKAHEAD = 3
PROJ_ROWS = 512
ATTN_TILE = 512
FFN_ROWS = 512
FFN_CHUNKS = ((0, 1024), (1024, 2048), (2048, FFN_HIDDEN))
VMEM_LIMIT_BYTES = 56 * 1024 * 1024

MLA_WIDTH = MLA_HEADS * LANES
MLA_V_WIDTH = MLA_HEADS * MLA_V_DIM
DIFF_WIDTH = DIFF_HEADS * LANES

_BF16 = jnp.bfloat16
_F32 = jnp.float32


def _rms(x, g):
    return x * lax.rsqrt(jnp.mean(x * x, axis=-1, keepdims=True) + EPS) * g


def _dot(a, b):
    return jnp.dot(a, b, preferred_element_type=_F32)


def _dot_nt(a, b):
    return lax.dot_general(a, b, (((1,), (1,)), ((), ())), preferred_element_type=_F32)


def _t5_bucket(rel):
    nb = REL_BUCKETS // 2
    max_exact = nb // 2
    ret = (rel > 0).astype(jnp.int32) * nb
    n = jnp.abs(rel)
    nf = jnp.maximum(n, 1).astype(jnp.float32)
    large = max_exact + (jnp.log(nf / max_exact) / math.log(REL_MAX_DIST / max_exact)
                         * (nb - max_exact)).astype(jnp.int32)
    large = jnp.minimum(large, nb - 1)
    return ret + jnp.where(n < max_exact, n, large)


def _bias_kernel(rel_ref, bdiag_ref, bcorner_ref, adiag_ref, acorner_ref, mask_ref):
    t = QCOLS
    key = lax.broadcasted_iota(jnp.int32, (t, t), 0) // CHUNK
    qry = lax.broadcasted_iota(jnp.int32, (t, t), 1) // CHUNK
    mask = jnp.where(key <= qry, 0.0, NEG_INF).astype(_F32)
    mask_ref[...] = mask
    bd = bdiag_ref[...]
    bc = bcorner_ref[...]
    far_bucket = REL_BUCKETS // 2 - 1
    for h in range(DIFF_HEADS):
        far = rel_ref[far_bucket, h]
        vd = jnp.zeros((t, t), _F32)
        vc = jnp.zeros((LANES, LANES), _F32)
        for b in range(REL_BUCKETS):
            val = rel_ref[b, h]
            vd = jnp.where(bd == b, val, vd)
            vc = jnp.where(bc == b, val, vc)
        adiag_ref[h] = (vd - far) * LOG2E + mask
        acorner_ref[h] = (vc - far) * LOG2E


def _bias_tables(rel_bias):
    t = QCOLS
    i = jnp.arange(t)
    bdiag = _t5_bucket(i[:, None] - i[None, :])
    c = jnp.arange(LANES)
    bcorner = _t5_bucket(c[:, None] - c[None, :] - LANES)
    vmem = pl.BlockSpec(memory_space=pltpu.VMEM)
    return pl.pallas_call(
        _bias_kernel,
        out_shape=(jax.ShapeDtypeStruct((DIFF_HEADS, t, t), _F32),
                   jax.ShapeDtypeStruct((DIFF_HEADS, LANES, LANES), _F32),
                   jax.ShapeDtypeStruct((t, t), _F32)),
        in_specs=[pl.BlockSpec(memory_space=pltpu.SMEM), vmem, vmem],
        out_specs=(vmem, vmem, vmem),
        name="bias_tables",
    )(rel_bias, bdiag, bcorner)


def _proj_kernel(x_ref, gattn_ref, wa_ref, wdq_ref, wdv_ref, gq_ref, wq_ref, gkv_ref, wk_ref, wv_ref,
                 cos_ref, sin_ref, cos_t_ref, sin_t_ref,
                 qt_ref, k_ref, vt_ref, dqt_ref, dk_ref, dvt_ref):
    rows = x_ref.shape[0]
    h = _rms(x_ref[...], gattn_ref[...]).astype(_BF16)
    pa = _dot(h, wa_ref[...])
    dk_ref[...] = pa[:, 512:1024].astype(_BF16)
    dqt_ref[0, 0] = (_dot_nt(wdq_ref[...], h) * (DIFF_SCALE * LOG2E)).astype(_BF16)
    dvt_ref[0, 0] = _dot_nt(wdv_ref[...], h).astype(_BF16)

    cq = _rms(pa[:, 0:256], gq_ref[...]).astype(_BF16)
    ckv = _rms(pa[:, 256:384], gkv_ref[...]).astype(_BF16)

    qt = _dot_nt(wq_ref[...], cq)
    cos_t = cos_t_ref[...]
    sin_t = sin_t_ref[...]
    parts = []
    for hd in range(MLA_HEADS):
        b0 = hd * LANES
        r0 = b0 + MLA_NOPE_DIM
        x1 = qt[r0:r0 + ROPE_HALF]
        x2 = qt[r0 + ROPE_HALF:r0 + MLA_ROPE_DIM]
        parts += [qt[b0:r0], x1 * cos_t - x2 * sin_t, x2 * cos_t + x1 * sin_t,
                  qt[r0 + MLA_ROPE_DIM:b0 + LANES]]
    qt_ref[0, 0] = (jnp.concatenate(parts, axis=0) * (MLA_SCALE * LOG2E)).astype(_BF16)

    cos = cos_ref[...]
    sin = sin_ref[...]
    lane = lax.broadcasted_iota(jnp.int32, (rows, LANES), 1)
    first = (lane >= MLA_NOPE_DIM) & (lane < MLA_NOPE_DIM + ROPE_HALF)
    kr = pa[:, 384:512]
    kr = kr * cos + jnp.where(first, pltpu.roll(kr, LANES - ROPE_HALF, 1), pltpu.roll(kr, ROPE_HALF, 1)) * sin
    kk = _dot(ckv, wk_ref[...])
    k_ref[...] = (kk + jnp.concatenate([kr] * MLA_HEADS, axis=1)).astype(_BF16)
    vt_ref[0, 0] = _dot_nt(wv_ref[...], ckv).astype(_BF16)


def _projections(x2d, g_attn, w_a, w_dq_t, w_dv_t, g_q, w_q_t, g_kv, w_k_p, w_v_t,
                 cos_l, sin_l, cos_t, sin_t, batch, seq):
    n = x2d.shape[0]
    r = PROJ_ROWS
    tps = seq // r

    def rows(width):
        return pl.BlockSpec((r, width), lambda i: (i, 0))

    def cols(height):
        return pl.BlockSpec((1, 1, height, r), lambda i: (i // tps, i % tps, 0, 0))

    def whole(a):
        return pl.BlockSpec(a.shape, lambda i: (0, 0))

    pos_rows = pl.BlockSpec((r, LANES), lambda i: (i % tps, 0))
    pos_cols = pl.BlockSpec((ROPE_HALF, r), lambda i: (0, i % tps))
    tok = lambda w: jax.ShapeDtypeStruct((n, w), _BF16)
    feat = lambda hgt: jax.ShapeDtypeStruct((batch, tps, hgt, r), _BF16)
    return pl.pallas_call(
        _proj_kernel,
        grid=(n // r,),
        in_specs=[rows(D_MODEL), whole(g_attn), whole(w_a), whole(w_dq_t), whole(w_dv_t), whole(g_q),
                  whole(w_q_t), whole(g_kv), whole(w_k_p), whole(w_v_t), pos_rows, pos_rows, pos_cols, pos_cols],
        out_specs=(cols(MLA_WIDTH), rows(MLA_WIDTH), cols(MLA_V_WIDTH), cols(DIFF_WIDTH), rows(DIFF_WIDTH),
                   cols(DIFF_WIDTH)),
        out_shape=(feat(MLA_WIDTH), tok(MLA_WIDTH), feat(MLA_V_WIDTH), feat(DIFF_WIDTH), tok(DIFF_WIDTH),
                   feat(DIFF_WIDTH)),
        compiler_params=pltpu.CompilerParams(dimension_semantics=("parallel",),
                                             vmem_limit_bytes=VMEM_LIMIT_BYTES),
        name="projections",
    )(x2d, g_attn, w_a, w_dq_t, w_dv_t, g_q, w_q_t, g_kv, w_k_p, w_v_t, cos_l, sin_l, cos_t, sin_t)


def _softmax_pv_step(s_t, v_t, m_ref, acc_ref, h, cols):
    m_prev = m_ref[h, :, cols]
    m_new = jnp.maximum(m_prev, jnp.max(s_t, axis=0, keepdims=True))
    alpha = jnp.exp2(m_prev - m_new)
    p_t = jnp.exp2(s_t - m_new).astype(_BF16)
    ones = (lax.broadcasted_iota(jnp.int32, (ONES_ROWS, v_t.shape[1]), 0) == 0).astype(_BF16)
    v_aug = jnp.concatenate([v_t, ones], axis=0)
    acc_ref[h, :, cols] = alpha * acc_ref[h, :, cols] + _dot(v_aug, p_t)
    m_ref[h, :, cols] = m_new


def _add_at(s, table, r0, c0):
    nr, nc = table.shape
    rows = s[r0:r0 + nr]
    pieces = [rows[:, :c0]] if c0 > 0 else []
    pieces.append(rows[:, c0:c0 + nc] + table)
    if c0 + nc < s.shape[1]:
        pieces.append(rows[:, c0 + nc:])
    rows = jnp.concatenate(pieces, axis=1) if len(pieces) > 1 else pieces[0]
    out = [s[:r0]] if r0 > 0 else []
    out.append(rows)
    if r0 + nr < s.shape[0]:
        out.append(s[r0 + nr:])
    return jnp.concatenate(out, axis=0) if len(out) > 1 else out[0]


def _attn_kernel(qt_ref, k_ref, vt_ref, dqt_ref, dk_ref, dvt_ref,
                 ablock_ref, acorner_ref, mask_ref, lam_ref, gsub_ref,
                 out_ref,
                 m_a, acc_a, m_d, acc_d):
    t = ATTN_TILE
    qc = QCOLS
    qi = pl.program_id(1)

    m_a[...] = jnp.full(m_a.shape, -jnp.inf, _F32)
    acc_a[...] = jnp.zeros(acc_a.shape, _F32)
    m_d[...] = jnp.full(m_d.shape, -jnp.inf, _F32)
    acc_d[...] = jnp.zeros(acc_d.shape, _F32)

    units = ([("mla", h, 0, c0) for h in range(MLA_HEADS) for c0 in range(0, t, qc)]
             + [("diff", h, part, c0) for h in range(DIFF_HEADS) for part in (0, 1) for c0 in range(0, t, qc)])

    def key_count(unit, kind):
        return qc if (kind == "diag" and unit[3] == 0) else t

    def scores(unit, kind, ki):
        head_kind, h, part, c0 = unit
        nkeys = key_count(unit, kind)
        blk = slice(h * LANES, (h + 1) * LANES)
        if head_kind == "mla":
            return _dot(k_ref[0, ki, :nkeys, blk], qt_ref[0, 0, blk, c0:c0 + qc])
        q = dqt_ref[0, 0, blk, c0:c0 + qc]
        feat = lax.broadcasted_iota(jnp.int32, (LANES, qc), 0)
        keep = (feat < DIFF_HEAD_DIM) if part == 0 else (feat >= DIFF_HEAD_DIM)
        return _dot(dk_ref[0, ki, :nkeys, blk], jnp.where(keep, q, jnp.zeros_like(q)))

    def softmax_pv(unit, s_t, kind, ki):
        head_kind, h, part, c0 = unit
        nkeys = key_count(unit, kind)
        if head_kind == "mla":
            if kind == "diag":
                s_t = _add_at(s_t, mask_ref[...], nkeys - qc, 0)
            v_t = vt_ref[0, ki, h * MLA_V_DIM:(h + 1) * MLA_V_DIM, :nkeys]
            _softmax_pv_step(s_t, v_t, m_a, acc_a, h, slice(c0, c0 + qc))
        else:
            if kind == "diag":
                s_t = _add_at(s_t, ablock_ref[h], nkeys - qc, 0)
                if c0 > 0:
                    s_t = _add_at(s_t, acorner_ref[h], nkeys - qc - LANES, 0)
            elif kind == "sub" and c0 == 0:
                s_t = _add_at(s_t, acorner_ref[h], t - LANES, 0)
            v_t = dvt_ref[0, ki, h * LANES:(h + 1) * LANES, :nkeys]
            _softmax_pv_step(s_t, v_t, m_d, acc_d, h, slice(part * t + c0, part * t + c0 + qc))

    def tile_step(kind, ki):
        pending = [scores(u, kind, ki) for u in units[:SCORE_LOOKAHEAD]]
        for i, unit in enumerate(units):
            if i + SCORE_LOOKAHEAD < len(units):
                pending.append(scores(units[i + SCORE_LOOKAHEAD], kind, ki))
            softmax_pv(unit, pending.pop(0), kind, ki)

    def far_body(ki, carry):
        tile_step("far", ki)
        return carry

    lax.fori_loop(0, jnp.maximum(qi - 1, 0), far_body, 0)

    @pl.when(qi >= 1)
    def _sub():
        tile_step("sub", qi - 1)

    tile_step("diag", qi)

    for j in range(MLA_HEADS // 2):
        a0 = acc_a[2 * j]
        a1 = acc_a[2 * j + 1]
        o_t = jnp.concatenate([a0[:MLA_V_DIM] * (1.0 / a0[MLA_V_DIM:MLA_V_DIM + 1]),
                               a1[:MLA_V_DIM] * (1.0 / a1[MLA_V_DIM:MLA_V_DIM + 1])], axis=0)
        out_ref[0, :, j * LANES:(j + 1) * LANES] = o_t.T.astype(_BF16)
    lam = (jnp.exp(jnp.sum(lam_ref[0:1, :] * lam_ref[1:2, :], axis=-1, keepdims=True))
           - jnp.exp(jnp.sum(lam_ref[2:3, :] * lam_ref[3:4, :], axis=-1, keepdims=True))
           + LAM_INIT)
    base = MLA_V_WIDTH
    for h in range(DIFF_HEADS):
        acc = acc_d[h]
        inv = 1.0 / acc[DIFF_V_DIM:DIFF_V_DIM + 1]
        o_t = acc[:DIFF_V_DIM, :t] * inv[:, :t] - lam * (acc[:DIFF_V_DIM, t:] * inv[:, t:])
        y = _rms(o_t.T, gsub_ref[...]) * (1.0 - LAM_INIT)
        out_ref[0, :, base + h * LANES:base + (h + 1) * LANES] = y.astype(_BF16)


def _attention(qt, k, vt, dqt, dk, dvt, adiag, acorner, mask, lam_vecs, g_sub):
    b, nk, t, _ = k.shape
    assert t == ATTN_TILE
    s = nk * t

    def q_tile(a):
        return pl.BlockSpec((1, 1) + a.shape[2:], lambda bi, qi: (bi, qi, 0, 0))

    def resident(a):
        return pl.BlockSpec((1,) + a.shape[1:], lambda bi, qi: (bi, 0, 0, 0), pipeline_mode=pl.Buffered(1))

    def const(a):
        nd = a.ndim
        return pl.BlockSpec(a.shape, lambda bi, qi: (0,) * nd, pipeline_mode=pl.Buffered(1))

    width = MLA_V_WIDTH + DIFF_WIDTH
    return pl.pallas_call(
        _attn_kernel,
        grid=(b, nk),
        in_specs=[q_tile(qt), resident(k), resident(vt), q_tile(dqt), resident(dk), resident(dvt),
                  const(adiag), const(acorner), const(mask), const(lam_vecs), const(g_sub)],
        out_specs=pl.BlockSpec((1, t, width), lambda bi, qi: (bi, qi, 0)),
        scratch_shapes=[
            pltpu.VMEM((MLA_HEADS, 1, t), _F32),
            pltpu.VMEM((MLA_HEADS, MLA_V_DIM + ONES_ROWS, t), _F32),
            pltpu.VMEM((DIFF_HEADS, 1, 2 * t), _F32),
            pltpu.VMEM((DIFF_HEADS, DIFF_V_DIM + ONES_ROWS, 2 * t), _F32),
        ],
        out_shape=jax.ShapeDtypeStruct((b, s, width), _BF16),
        compiler_params=pltpu.CompilerParams(dimension_semantics=("parallel", "arbitrary"),
                                             vmem_limit_bytes=VMEM_LIMIT_BYTES),
        name="attention",
    )(qt, k, vt, dqt, dk, dvt, adiag, acorner, mask, lam_vecs, g_sub)


def _ffn_kernel(mixed_ref, x_ref, wo_ref, gffn_ref, wg_ref, wu_ref, wd_ref, gfin_ref, out_ref):
    x1 = x_ref[...] + _dot(mixed_ref[...], wo_ref[...])
    h = _rms(x1, gffn_ref[...]).astype(_BF16)
    y = jnp.zeros_like(x1)
    for c0, c1 in FFN_CHUNKS:
        g = _dot(h, wg_ref[:, c0:c1])
        u = _dot(h, wu_ref[:, c0:c1])
        act = (g * (1.0 / (1.0 + jnp.exp(-g))) * u).astype(_BF16)
        y = y + _dot(act, wd_ref[c0:c1, :])
    out_ref[...] = _rms(x1 + y, gfin_ref[...])


def _out_ffn(mixed2d, x2d, w_o, g_ffn, w_g, w_u, w_d, g_fin):
    n = x2d.shape[0]
    r = FFN_ROWS

    def rows(width):
        return pl.BlockSpec((r, width), lambda i: (i, 0))

    def resident(a):
        return pl.BlockSpec(a.shape, lambda i: (0, 0), pipeline_mode=pl.Buffered(1))

    return pl.pallas_call(
        _ffn_kernel,
        grid=(n // r,),
        in_specs=[rows(mixed2d.shape[1]), rows(D_MODEL), resident(w_o), resident(g_ffn), resident(w_g),
                  resident(w_u), resident(w_d), resident(g_fin)],
        out_specs=rows(D_MODEL),
        out_shape=jax.ShapeDtypeStruct((n, D_MODEL), _F32),
        compiler_params=pltpu.CompilerParams(dimension_semantics=("parallel",),
                                             vmem_limit_bytes=VMEM_LIMIT_BYTES),
        name="out_ffn",
    )(mixed2d, x2d, w_o, g_ffn, w_g, w_u, w_d, g_fin)


def _rope_tables(seq):
    inv_freq = ROPE_THETA ** (-jnp.arange(ROPE_HALF, dtype=jnp.float32) / ROPE_HALF)
    ang = jnp.arange(seq).astype(jnp.float32)[:, None] * inv_freq[None, :]
    cos = jnp.cos(ang)
    sin = jnp.sin(ang)
    one = jnp.ones((seq, MLA_NOPE_DIM), _F32)
    zero = jnp.zeros((seq, MLA_NOPE_DIM), _F32)
    pad = LANES - MLA_NOPE_DIM - MLA_ROPE_DIM
    cos_l = jnp.concatenate([one, cos, cos, one[:, :pad]], axis=1)
    sin_l = jnp.concatenate([zero, -sin, sin, zero[:, :pad]], axis=1)
    return cos_l, sin_l, cos.T, sin.T


def kernel(x, w_in, g_attn, g_q_a, w_q_b, g_kv_a, w_kv_b, lam_q1, lam_k1, lam_q2, lam_k2, g_subln,
           rel_bias, w_o, g_ffn, w_ffn_gate, w_ffn_up, w_ffn_down, g_final):
    b, s, d = x.shape
    assert d == D_MODEL and s % ATTN_TILE == 0 and PROJ_ROWS == ATTN_TILE and (b * s) % FFN_ROWS == 0
    assert w_in.shape[0] == 1, "single layer"

    w = w_in[0]
    zc = lambda n: jnp.zeros((D_MODEL, n), w.dtype)
    pad = LANES - MLA_NOPE_DIM - MLA_ROPE_DIM
    w_a = jnp.concatenate([w[:, 0:384], zc(MLA_NOPE_DIM), w[:, 384:416], zc(pad), w[:, 928:1440]],
                          axis=1).astype(_BF16)
    w_dq_t = w[:, 416:928].T.astype(_BF16)
    w_dv_t = w[:, 1440:1952].T.astype(_BF16)
    wq = w_q_b[0].reshape(MLA_Q_RANK, MLA_HEADS, MLA_NOPE_DIM + MLA_ROPE_DIM)
    w_q_t = jnp.concatenate([wq, jnp.zeros((MLA_Q_RANK, MLA_HEADS, pad), wq.dtype)],
                            axis=2).reshape(MLA_Q_RANK, MLA_WIDTH).T.astype(_BF16)
    wkv = w_kv_b[0].reshape(MLA_KV_RANK, MLA_HEADS, MLA_NOPE_DIM + MLA_V_DIM)
    w_k_p = jnp.concatenate([wkv[:, :, :MLA_NOPE_DIM],
                             jnp.zeros((MLA_KV_RANK, MLA_HEADS, LANES - MLA_NOPE_DIM), wkv.dtype)],
                            axis=2).reshape(MLA_KV_RANK, MLA_WIDTH).astype(_BF16)
    w_v_t = wkv[:, :, MLA_NOPE_DIM:].reshape(MLA_KV_RANK, MLA_V_WIDTH).T.astype(_BF16)

    cos_l, sin_l, cos_t, sin_t = _rope_tables(s)
    x2d = x.reshape(b * s, d)
    qt, k, vt, dqt, dk, dvt = _projections(
        x2d, g_attn[0][None, :], w_a, w_dq_t, w_dv_t, g_q_a[0][None, :], w_q_t, g_kv_a[0][None, :],
        w_k_p, w_v_t, cos_l, sin_l, cos_t, sin_t, b, s)

    adiag, acorner, mask = _bias_tables(rel_bias)
    lam_vecs = jnp.stack([lam_q1[0], lam_k1[0], lam_q2[0], lam_k2[0]]).astype(_F32)
    nk = s // ATTN_TILE
    mixed = _attention(qt, k.reshape(b, nk, ATTN_TILE, MLA_WIDTH), vt, dqt,
                       dk.reshape(b, nk, ATTN_TILE, DIFF_WIDTH), dvt,
                       adiag, acorner, mask, lam_vecs, g_subln[0][None, :])

    out = _out_ffn(mixed.reshape(b * s, MLA_V_WIDTH + DIFF_WIDTH), x2d, w_o[0].astype(_BF16),
                   g_ffn[0][None, :], w_ffn_gate[0].astype(_BF16), w_ffn_up[0].astype(_BF16),
                   w_ffn_down[0].astype(_BF16), g_final[None, :])
    return out.reshape(b, s, d)
```

```python
import math

import jax
import jax.numpy as jnp
from jax import lax
from jax.experimental import pallas as pl
from jax.experimental.pallas import tpu as pltpu

D_MODEL = 1024
CHUNK = 64
EPS = 1e-6
NEG_INF = -1e30

MLA_HEADS = 8
MLA_NOPE_DIM = 64
MLA_ROPE_DIM = 32
MLA_V_DIM = 64
MLA_Q_RANK = 256
MLA_KV_RANK = 128
ROPE_THETA = 10000.0
ROPE_HALF = MLA_ROPE_DIM // 2

DIFF_HEADS = 4
DIFF_HEAD_DIM = 64
DIFF_V_DIM = 2 * DIFF_HEAD_DIM

REL_BUCKETS = 32
REL_MAX_DIST = 128

FFN_HIDDEN = 2816
LAM_INIT = 0.8 - 0.6 * math.exp(-0.3 * 0)

LOG2E = 1.4426950408889634
MLA_SCALE = (MLA_NOPE_DIM + MLA_ROPE_DIM) ** -0.5
DIFF_SCALE = DIFF_HEAD_DIM ** -0.5

LANES = 128
ONES_ROWS = 16
QCOLS = 256
SCORE_LOOKAHEAD = 3
PROJ_ROWS = 512
ATTN_TILE = 512
FFN_ROWS = 512
FFN_CHUNKS = ((0, 1024), (1024, 2048), (2048, FFN_HIDDEN))
VMEM_LIMIT_BYTES = 56 * 1024 * 1024

MLA_WIDTH = MLA_HEADS * LANES
MLA_V_WIDTH = MLA_HEADS * MLA_V_DIM
DIFF_WIDTH = DIFF_HEADS * LANES

_BF16 = jnp.bfloat16
_F32 = jnp.float32


def _rms(x, g):
    return x * lax.rsqrt(jnp.mean(x * x, axis=-1, keepdims=True) + EPS) * g


def _dot(a, b):
    return jnp.dot(a, b, preferred_element_type=_F32)


def _dot_nt(a, b):
    return lax.dot_general(a, b, (((1,), (1,)), ((), ())), preferred_element_type=_F32)


def _t5_bucket(rel):
    nb = REL_BUCKETS // 2
    max_exact = nb // 2
    ret = (rel > 0).astype(jnp.int32) * nb
    n = jnp.abs(rel)
    nf = jnp.maximum(n, 1).astype(jnp.float32)
    large = max_exact + (jnp.log(nf / max_exact) / math.log(REL_MAX_DIST / max_exact)
                         * (nb - max_exact)).astype(jnp.int32)
    large = jnp.minimum(large, nb - 1)
    return ret + jnp.where(n < max_exact, n, large)


def _bias_kernel(rel_ref, bdiag_ref, bcorner_ref, adiag_ref, acorner_ref, mask_ref):
    t = QCOLS
    key = lax.broadcasted_iota(jnp.int32, (t, t), 0) // CHUNK
    qry = lax.broadcasted_iota(jnp.int32, (t, t), 1) // CHUNK
    mask = jnp.where(key <= qry, 0.0, NEG_INF).astype(_F32)
    mask_ref[...] = mask
    bd = bdiag_ref[...]
    bc = bcorner_ref[...]
    far_bucket = REL_BUCKETS // 2 - 1
    for h in range(DIFF_HEADS):
        far = rel_ref[far_bucket, h]
        vd = jnp.zeros((t, t), _F32)
        vc = jnp.zeros((LANES, LANES), _F32)
        for b in range(REL_BUCKETS):
            val = rel_ref[b, h]
            vd = jnp.where(bd == b, val, vd)
            vc = jnp.where(bc == b, val, vc)
        adiag_ref[h] = (vd - far) * LOG2E + mask
        acorner_ref[h] = (vc - far) * LOG2E


def _bias_tables(rel_bias):
    t = QCOLS
    i = jnp.arange(t)
    bdiag = _t5_bucket(i[:, None] - i[None, :])
    c = jnp.arange(LANES)
    bcorner = _t5_bucket(c[:, None] - c[None, :] - LANES)
    vmem = pl.BlockSpec(memory_space=pltpu.VMEM)
    return pl.pallas_call(
        _bias_kernel,
        out_shape=(jax.ShapeDtypeStruct((DIFF_HEADS, t, t), _F32),
                   jax.ShapeDtypeStruct((DIFF_HEADS, LANES, LANES), _F32),
                   jax.ShapeDtypeStruct((t, t), _F32)),
        in_specs=[pl.BlockSpec(memory_space=pltpu.SMEM), vmem, vmem],
        out_specs=(vmem, vmem, vmem),
        name="bias_tables",
    )(rel_bias, bdiag, bcorner)


def _proj_kernel(x_ref, gattn_ref, wa_ref, wdq_ref, wdv_ref, gq_ref, wq_ref, gkv_ref, wk_ref, wv_ref,
                 cos_ref, sin_ref, cos_t_ref, sin_t_ref,
                 qt_ref, k_ref, vt_ref, dqt_ref, dk_ref, dvt_ref):
    rows = x_ref.shape[0]
    h = _rms(x_ref[...], gattn_ref[...]).astype(_BF16)
    pa = _dot(h, wa_ref[...])
    for hd in range(DIFF_HEADS):
        dk_ref[0, 0, hd] = pa[:, 512 + hd * LANES:512 + (hd + 1) * LANES].astype(_BF16)
    dqt_ref[0, 0] = (_dot_nt(wdq_ref[...], h) * (DIFF_SCALE * LOG2E)).astype(_BF16)
    dvt_ref[0, 0] = _dot_nt(wdv_ref[...], h).astype(_BF16)

    cq = _rms(pa[:, 0:256], gq_ref[...]).astype(_BF16)
    ckv = _rms(pa[:, 256:384], gkv_ref[...]).astype(_BF16)

    qt = _dot_nt(wq_ref[...], cq)
    cos_t = cos_t_ref[...]
    sin_t = sin_t_ref[...]
    parts = []
    for hd in range(MLA_HEADS):
        b0 = hd * LANES
        r0 = b0 + MLA_NOPE_DIM
        x1 = qt[r0:r0 + ROPE_HALF]
        x2 = qt[r0 + ROPE_HALF:r0 + MLA_ROPE_DIM]
        parts += [qt[b0:r0], x1 * cos_t - x2 * sin_t, x2 * cos_t + x1 * sin_t,
                  qt[r0 + MLA_ROPE_DIM:b0 + LANES]]
    qt_ref[0, 0] = (jnp.concatenate(parts, axis=0) * (MLA_SCALE * LOG2E)).astype(_BF16)

    cos = cos_ref[...]
    sin = sin_ref[...]
    lane = lax.broadcasted_iota(jnp.int32, (rows, LANES), 1)
    first = (lane >= MLA_NOPE_DIM) & (lane < MLA_NOPE_DIM + ROPE_HALF)
    kr = pa[:, 384:512]
    kr = kr * cos + jnp.where(first, pltpu.roll(kr, LANES - ROPE_HALF, 1), pltpu.roll(kr, ROPE_HALF, 1)) * sin
    kk = _dot(ckv, wk_ref[...])
    for hd in range(MLA_HEADS):
        k_ref[0, 0, hd] = (kk[:, hd * LANES:(hd + 1) * LANES] + kr).astype(_BF16)
    vt_ref[0, 0] = _dot_nt(wv_ref[...], ckv).astype(_BF16)


def _projections(x2d, g_attn, w_a, w_dq_t, w_dv_t, g_q, w_q_t, g_kv, w_k_p, w_v_t,
                 cos_l, sin_l, cos_t, sin_t, batch, seq):
    n = x2d.shape[0]
    r = PROJ_ROWS
    tps = seq // r

    def rows(width):
        return pl.BlockSpec((r, width), lambda i: (i, 0))

    def cols(height):
        return pl.BlockSpec((1, 1, height, r), lambda i: (i // tps, i % tps, 0, 0))

    def whole(a):
        return pl.BlockSpec(a.shape, lambda i: (0, 0))

    pos_rows = pl.BlockSpec((r, LANES), lambda i: (i % tps, 0))
    pos_cols = pl.BlockSpec((ROPE_HALF, r), lambda i: (0, i % tps))
    def slabs(heads):
        return pl.BlockSpec((1, 1, heads, r, LANES), lambda i: (i // tps, i % tps, 0, 0, 0))

    tok = lambda heads: jax.ShapeDtypeStruct((batch, tps, heads, r, LANES), _BF16)
    feat = lambda hgt: jax.ShapeDtypeStruct((batch, tps, hgt, r), _BF16)
    return pl.pallas_call(
        _proj_kernel,
        grid=(n // r,),
        in_specs=[rows(D_MODEL), whole(g_attn), whole(w_a), whole(w_dq_t), whole(w_dv_t), whole(g_q),
                  whole(w_q_t), whole(g_kv), whole(w_k_p), whole(w_v_t), pos_rows, pos_rows, pos_cols, pos_cols],
        out_specs=(cols(MLA_WIDTH), slabs(MLA_HEADS), cols(MLA_V_WIDTH), cols(DIFF_WIDTH), slabs(DIFF_HEADS),
                   cols(DIFF_WIDTH)),
        out_shape=(feat(MLA_WIDTH), tok(MLA_HEADS), feat(MLA_V_WIDTH), feat(DIFF_WIDTH), tok(DIFF_HEADS),
                   feat(DIFF_WIDTH)),
        compiler_params=pltpu.CompilerParams(dimension_semantics=("parallel",),
                                             vmem_limit_bytes=VMEM_LIMIT_BYTES),
        name="projections",
    )(x2d, g_attn, w_a, w_dq_t, w_dv_t, g_q, w_q_t, g_kv, w_k_p, w_v_t, cos_l, sin_l, cos_t, sin_t)


def _softmax_pv_step(s_t, v_t, m_ref, acc_ref, h, cols):
    m_prev = m_ref[h, :, cols]
    m_new = jnp.maximum(m_prev, jnp.max(s_t, axis=0, keepdims=True))
    alpha = jnp.exp2(m_prev - m_new)
    p_t = jnp.exp2(s_t - m_new).astype(_BF16)
    ones = (lax.broadcasted_iota(jnp.int32, (ONES_ROWS, v_t.shape[1]), 0) == 0).astype(_BF16)
    v_aug = jnp.concatenate([v_t, ones], axis=0)
    acc_ref[h, :, cols] = alpha * acc_ref[h, :, cols] + _dot(v_aug, p_t)
    m_ref[h, :, cols] = m_new


def _add_at(s, table, r0, c0):
    nr, nc = table.shape
    rows = s[r0:r0 + nr]
    pieces = [rows[:, :c0]] if c0 > 0 else []
    pieces.append(rows[:, c0:c0 + nc] + table)
    if c0 + nc < s.shape[1]:
        pieces.append(rows[:, c0 + nc:])
    rows = jnp.concatenate(pieces, axis=1) if len(pieces) > 1 else pieces[0]
    out = [s[:r0]] if r0 > 0 else []
    out.append(rows)
    if r0 + nr < s.shape[0]:
        out.append(s[r0 + nr:])
    return jnp.concatenate(out, axis=0) if len(out) > 1 else out[0]


def _attn_kernel(qt_ref, k_ref, vt_ref, dqt_ref, dk_ref, dvt_ref,
                 ablock_ref, acorner_ref, mask_ref, lam_ref, gsub_ref,
                 out_ref,
                 m_a, acc_a, m_d, acc_d, s_pend):
    t = ATTN_TILE
    qc = QCOLS
    qi = pl.program_id(1)

    m_a[...] = jnp.full(m_a.shape, -jnp.inf, _F32)
    acc_a[...] = jnp.zeros(acc_a.shape, _F32)
    m_d[...] = jnp.full(m_d.shape, -jnp.inf, _F32)
    acc_d[...] = jnp.zeros(acc_d.shape, _F32)

    units = ([("mla", h, 0, c0) for h in range(MLA_HEADS) for c0 in range(0, t, qc)]
             + [("diff", h, part, c0) for h in range(DIFF_HEADS) for part in (0, 1) for c0 in range(0, t, qc)])

    def key_count(unit, kind):
        return qc if (kind == "diag" and unit[3] == 0) else t

    def scores(unit, kind, ki):
        head_kind, h, part, c0 = unit
        nkeys = key_count(unit, kind)
        blk = slice(h * LANES, (h + 1) * LANES)
        if head_kind == "mla":
            return _dot(k_ref[0, ki, h, :nkeys, :], qt_ref[0, 0, blk, c0:c0 + qc])
        q = dqt_ref[0, 0, blk, c0:c0 + qc]
        feat = lax.broadcasted_iota(jnp.int32, (LANES, qc), 0)
        keep = (feat < DIFF_HEAD_DIM) if part == 0 else (feat >= DIFF_HEAD_DIM)
        return _dot(dk_ref[0, ki, h, :nkeys, :], jnp.where(keep, q, jnp.zeros_like(q)))

    def softmax_pv(unit, s_t, kind, ki):
        head_kind, h, part, c0 = unit
        nkeys = key_count(unit, kind)
        if head_kind == "mla":
            if kind == "diag":
                s_t = _add_at(s_t, mask_ref[...], nkeys - qc, 0)
            v_t = vt_ref[0, ki, h * MLA_V_DIM:(h + 1) * MLA_V_DIM, :nkeys]
            _softmax_pv_step(s_t, v_t, m_a, acc_a, h, slice(c0, c0 + qc))
        else:
            if kind == "diag":
                s_t = _add_at(s_t, ablock_ref[h], nkeys - qc, 0)
                if c0 > 0:
                    s_t = _add_at(s_t, acorner_ref[h], nkeys - qc - LANES, 0)
            elif kind == "sub" and c0 == 0:
                s_t = _add_at(s_t, acorner_ref[h], t - LANES, 0)
            v_t = dvt_ref[0, ki, h * LANES:(h + 1) * LANES, :nkeys]
            _softmax_pv_step(s_t, v_t, m_d, acc_d, h, slice(part * t + c0, part * t + c0 + qc))

    def tile_step(kind, ki, carried_in, carry_out):
        n = len(units)
        if carried_in:
            pending = [s_pend[u] for u in range(SCORE_LOOKAHEAD)]
        else:
            pending = [scores(u, kind, ki) for u in units[:SCORE_LOOKAHEAD]]
        for i, unit in enumerate(units):
            j = i + SCORE_LOOKAHEAD
            if j < n:
                pending.append(scores(units[j], kind, ki))
            elif carry_out:
                s_pend[j - n] = scores(units[j - n], "far", ki + 1)
            softmax_pv(unit, pending.pop(0), kind, ki)

    @pl.when(qi >= 1)
    def _prologue():
        for u in range(SCORE_LOOKAHEAD):
            s_pend[u] = scores(units[u], "far", 0)

    def far_body(ki, carry):
        tile_step("far", ki, True, True)
        return carry

    lax.fori_loop(0, jnp.maximum(qi - 1, 0), far_body, 0)

    @pl.when(qi >= 1)
    def _sub():
        tile_step("sub", qi - 1, True, False)

    tile_step("diag", qi, False, False)

    for j in range(MLA_HEADS // 2):
        a0 = acc_a[2 * j]
        a1 = acc_a[2 * j + 1]
        o_t = jnp.concatenate([a0[:MLA_V_DIM] * (1.0 / a0[MLA_V_DIM:MLA_V_DIM + 1]),
                               a1[:MLA_V_DIM] * (1.0 / a1[MLA_V_DIM:MLA_V_DIM + 1])], axis=0)
        out_ref[0, :, j * LANES:(j + 1) * LANES] = o_t.T.astype(_BF16)
    lam = (jnp.exp(jnp.sum(lam_ref[0:1, :] * lam_ref[1:2, :], axis=-1, keepdims=True))
           - jnp.exp(jnp.sum(lam_ref[2:3, :] * lam_ref[3:4, :], axis=-1, keepdims=True))
           + LAM_INIT)
    base = MLA_V_WIDTH
    for h in range(DIFF_HEADS):
        acc = acc_d[h]
        inv = 1.0 / acc[DIFF_V_DIM:DIFF_V_DIM + 1]
        o_t = acc[:DIFF_V_DIM, :t] * inv[:, :t] - lam * (acc[:DIFF_V_DIM, t:] * inv[:, t:])
        y = _rms(o_t.T, gsub_ref[...]) * (1.0 - LAM_INIT)
        out_ref[0, :, base + h * LANES:base + (h + 1) * LANES] = y.astype(_BF16)


def _attention(qt, k, vt, dqt, dk, dvt, adiag, acorner, mask, lam_vecs, g_sub):
    b, nk, _, t, _ = k.shape
    assert t == ATTN_TILE
    s = nk * t

    def q_tile(a):
        return pl.BlockSpec((1, 1) + a.shape[2:], lambda bi, qi: (bi, qi, 0, 0))

    def resident(a):
        nd = a.ndim
        return pl.BlockSpec((1,) + a.shape[1:], lambda bi, qi: (bi,) + (0,) * (nd - 1),
                            pipeline_mode=pl.Buffered(1))

    def const(a):
        nd = a.ndim
        return pl.BlockSpec(a.shape, lambda bi, qi: (0,) * nd, pipeline_mode=pl.Buffered(1))

    width = MLA_V_WIDTH + DIFF_WIDTH
    return pl.pallas_call(
        _attn_kernel,
        grid=(b, nk),
        in_specs=[q_tile(qt), resident(k), resident(vt), q_tile(dqt), resident(dk), resident(dvt),
                  const(adiag), const(acorner), const(mask), const(lam_vecs), const(g_sub)],
        out_specs=pl.BlockSpec((1, t, width), lambda bi, qi: (bi, qi, 0)),
        scratch_shapes=[
            pltpu.VMEM((MLA_HEADS, 1, t), _F32),
            pltpu.VMEM((MLA_HEADS, MLA_V_DIM + ONES_ROWS, t), _F32),
            pltpu.VMEM((DIFF_HEADS, 1, 2 * t), _F32),
            pltpu.VMEM((DIFF_HEADS, DIFF_V_DIM + ONES_ROWS, 2 * t), _F32),
            pltpu.VMEM((SCORE_LOOKAHEAD, t, QCOLS), _F32),
        ],
        out_shape=jax.ShapeDtypeStruct((b, s, width), _BF16),
        compiler_params=pltpu.CompilerParams(dimension_semantics=("parallel", "arbitrary"),
                                             vmem_limit_bytes=VMEM_LIMIT_BYTES),
        name="attention",
    )(qt, k, vt, dqt, dk, dvt, adiag, acorner, mask, lam_vecs, g_sub)


def _ffn_kernel(mixed_ref, x_ref, wo_ref, gffn_ref, wg_ref, wu_ref, wd_ref, gfin_ref, out_ref):
    x1 = x_ref[...] + _dot(mixed_ref[...], wo_ref[...])
    h = _rms(x1, gffn_ref[...]).astype(_BF16)
    y = jnp.zeros_like(x1)
    for c0, c1 in FFN_CHUNKS:
        g = _dot(h, wg_ref[:, c0:c1])
        u = _dot(h, wu_ref[:, c0:c1])
        act = (g * (1.0 / (1.0 + jnp.exp(-g))) * u).astype(_BF16)
        y = y + _dot(act, wd_ref[c0:c1, :])
    out_ref[...] = _rms(x1 + y, gfin_ref[...])


def _out_ffn(mixed2d, x2d, w_o, g_ffn, w_g, w_u, w_d, g_fin):
    n = x2d.shape[0]
    r = FFN_ROWS

    def rows(width):
        return pl.BlockSpec((r, width), lambda i: (i, 0))

    def resident(a):
        return pl.BlockSpec(a.shape, lambda i: (0, 0), pipeline_mode=pl.Buffered(1))

    return pl.pallas_call(
        _ffn_kernel,
        grid=(n // r,),
        in_specs=[rows(mixed2d.shape[1]), rows(D_MODEL), resident(w_o), resident(g_ffn), resident(w_g),
                  resident(w_u), resident(w_d), resident(g_fin)],
        out_specs=rows(D_MODEL),
        out_shape=jax.ShapeDtypeStruct((n, D_MODEL), _F32),
        compiler_params=pltpu.CompilerParams(dimension_semantics=("parallel",),
                                             vmem_limit_bytes=VMEM_LIMIT_BYTES),
        name="out_ffn",
    )(mixed2d, x2d, w_o, g_ffn, w_g, w_u, w_d, g_fin)


def _rope_tables(seq):
    inv_freq = ROPE_THETA ** (-jnp.arange(ROPE_HALF, dtype=jnp.float32) / ROPE_HALF)
    ang = jnp.arange(seq).astype(jnp.float32)[:, None] * inv_freq[None, :]
    cos = jnp.cos(ang)
    sin = jnp.sin(ang)
    one = jnp.ones((seq, MLA_NOPE_DIM), _F32)
    zero = jnp.zeros((seq, MLA_NOPE_DIM), _F32)
    pad = LANES - MLA_NOPE_DIM - MLA_ROPE_DIM
    cos_l = jnp.concatenate([one, cos, cos, one[:, :pad]], axis=1)
    sin_l = jnp.concatenate([zero, -sin, sin, zero[:, :pad]], axis=1)
    return cos_l, sin_l, cos.T, sin.T


def kernel(x, w_in, g_attn, g_q_a, w_q_b, g_kv_a, w_kv_b, lam_q1, lam_k1, lam_q2, lam_k2, g_subln,
           rel_bias, w_o, g_ffn, w_ffn_gate, w_ffn_up, w_ffn_down, g_final):
    b, s, d = x.shape
    assert d == D_MODEL and s % ATTN_TILE == 0 and PROJ_ROWS == ATTN_TILE and (b * s) % FFN_ROWS == 0
    assert w_in.shape[0] == 1, "single layer"

    w = w_in[0]
    zc = lambda n: jnp.zeros((D_MODEL, n), w.dtype)
    pad = LANES - MLA_NOPE_DIM - MLA_ROPE_DIM
    w_a = jnp.concatenate([w[:, 0:384], zc(MLA_NOPE_DIM), w[:, 384:416], zc(pad), w[:, 928:1440]],
                          axis=1).astype(_BF16)
    w_dq_t = w[:, 416:928].T.astype(_BF16)
    w_dv_t = w[:, 1440:1952].T.astype(_BF16)
    wq = w_q_b[0].reshape(MLA_Q_RANK, MLA_HEADS, MLA_NOPE_DIM + MLA_ROPE_DIM)
    w_q_t = jnp.concatenate([wq, jnp.zeros((MLA_Q_RANK, MLA_HEADS, pad), wq.dtype)],
                            axis=2).reshape(MLA_Q_RANK, MLA_WIDTH).T.astype(_BF16)
    wkv = w_kv_b[0].reshape(MLA_KV_RANK, MLA_HEADS, MLA_NOPE_DIM + MLA_V_DIM)
    w_k_p = jnp.concatenate([wkv[:, :, :MLA_NOPE_DIM],
                             jnp.zeros((MLA_KV_RANK, MLA_HEADS, LANES - MLA_NOPE_DIM), wkv.dtype)],
                            axis=2).reshape(MLA_KV_RANK, MLA_WIDTH).astype(_BF16)
    w_v_t = wkv[:, :, MLA_NOPE_DIM:].reshape(MLA_KV_RANK, MLA_V_WIDTH).T.astype(_BF16)

    cos_l, sin_l, cos_t, sin_t = _rope_tables(s)
    x2d = x.reshape(b * s, d)
    qt, k, vt, dqt, dk, dvt = _projections(
        x2d, g_attn[0][None, :], w_a, w_dq_t, w_dv_t, g_q_a[0][None, :], w_q_t, g_kv_a[0][None, :],
        w_k_p, w_v_t, cos_l, sin_l, cos_t, sin_t, b, s)

    adiag, acorner, mask = _bias_tables(rel_bias)
    lam_vecs = jnp.stack([lam_q1[0], lam_k1[0], lam_q2[0], lam_k2[0]]).astype(_F32)
    mixed = _attention(qt, k, vt, dqt, dk, dvt, adiag, acorner, mask, lam_vecs, g_subln[0][None, :])

    out = _out_ffn(mixed.reshape(b * s, MLA_V_WIDTH + DIFF_WIDTH), x2d, w_o[0].astype(_BF16),
                   g_ffn[0][None, :], w_ffn_gate[0].astype(_BF16), w_ffn_up[0].astype(_BF16),
                   w_ffn_down[0].astype(_BF16), g_final[None, :])
    return out.reshape(b, s, d)
```
